```python
import math
import jax, jax.numpy as jnp
from jax import lax
import numpy as np

D_MODEL = 2048
BATCH = 1
SEQ = 8192
DEPTH = 2
DEC_BATCH = 128
DEC_SEQ = 4
PAST_LEN = 2048
PAGE_SIZE = 128

ATTN_WIDTH = D_MODEL // 2
SSM_WIDTH = D_MODEL - ATTN_WIDTH
HEAD_DIM = 64
N_HEADS = ATTN_WIDTH // HEAD_DIM
DILATED_BRANCHES = ((128, 1), (512, 4), (2048, 16))
MAX_WINDOW = 2048
SSM_GROUP = 16
N_SSM_GROUPS = SSM_WIDTH // SSM_GROUP
SSM_STATE = 64
IN_COLS = 3 * ATTN_WIDTH + SSM_WIDTH
N_EXPERTS = 16
N_EXPERT_GROUPS = 4
EXPERTS_PER_GROUP = N_EXPERTS // N_EXPERT_GROUPS
TOP_K = 2
D_EXPERT = D_MODEL // 2
DEEPNORM_ALPHA = (2.0 * DEPTH) ** 0.25
DEEPNORM_BETA = (8.0 * DEPTH) ** -0.25
NORM_EPS = 1e-5

kernel_name = "hymba_longnet_s5_sharedrouter_moe_step"


def layer_norm(x, g, b):
    xf = x.astype(jnp.float32)
    mu = jnp.mean(xf, -1, keepdims=True)
    var = jnp.mean(jnp.square(xf - mu), -1, keepdims=True)
    return ((xf - mu) * lax.rsqrt(var + NORM_EPS) * g.astype(jnp.float32) + b.astype(jnp.float32)).astype(x.dtype)


def rms_norm(x, g):
    xf = x.astype(jnp.float32)
    return (xf * lax.rsqrt(jnp.mean(xf * xf, -1, keepdims=True) + NORM_EPS) * g.astype(jnp.float32)).astype(x.dtype)


def alibi_slopes():
    h = jnp.arange(1, N_HEADS + 1, dtype=jnp.float32)
    return jnp.exp2(-8.0 * h / N_HEADS)


def project_in(x, w_in):
    z = jnp.einsum('bld,dc->blc', x, w_in)
    q, k, v, u = jnp.split(z, [ATTN_WIDTH, 2 * ATTN_WIDTH, 3 * ATTN_WIDTH], axis=-1)
    heads = lambda a: a.reshape(a.shape[0], a.shape[1], N_HEADS, HEAD_DIM)
    return heads(q), heads(k), heads(v), u


def band_attend(q, k, v, lookback, dist_scale, slopes):
    f32 = jnp.float32
    nseq, n, h, dh = q.shape
    blk = lookback
    nb = -(-n // blk)
    pad = nb * blk - n

    def blocks(a):
        a = jnp.pad(a.astype(f32), ((0, 0), (0, pad), (0, 0), (0, 0)))
        return a.reshape(nseq, nb, blk, h, dh)

    def with_prev(a):
        prev = jnp.pad(a[:, :-1], ((0, 0), (1, 0), (0, 0), (0, 0), (0, 0)))
        return jnp.concatenate([prev, a], axis=2)

    qb = blocks(q)
    kk = with_prev(blocks(k))
    vv = with_prev(blocks(v))
    s = jnp.einsum('nbqhd,nbkhd->nbhqk', qb, kk) * (1.0 / math.sqrt(dh))
    qi = jnp.arange(blk)[:, None]
    kj = jnp.arange(2 * blk)[None, :]
    dist = qi + blk - kj
    first = (jnp.arange(nb) == 0)[:, None, None]
    valid = (dist >= 0) & (dist <= lookback) & ~(first & (kj < blk))[...]
    s = s - slopes[None, None, :, None, None] * (dist_scale * dist).astype(f32)[None, None, None]
    s = jnp.where(valid[None, :, None], s, -jnp.inf)
    m = jnp.max(s, -1)
    p = jnp.exp(s - m[..., None])
    l = jnp.sum(p, -1)
    o = jnp.einsum('nbhqk,nbkhd->nbqhd', p, vv).reshape(nseq, nb * blk, h, dh)[:, :n]
    l = jnp.transpose(l, (0, 1, 3, 2)).reshape(nseq, nb * blk, h)[:, :n]
    m = jnp.transpose(m, (0, 1, 3, 2)).reshape(nseq, nb * blk, h)[:, :n]
    return o, l, m


def dilated_branch_prompt(q, k, v, window, dilation, slopes):
    b, L, h, dh = q.shape
    n = L // dilation

    def split(a):
        return jnp.transpose(a.reshape(b, n, dilation, h, dh), (0, 2, 1, 3, 4)).reshape(b * dilation, n, h, dh)

    def merge(a):
        rest = a.shape[2:]
        a = jnp.moveaxis(a.reshape(b, dilation, n, *rest), 1, 2)
        return a.reshape(b, L, *rest)

    o, l, m = band_attend(split(q), split(k), split(v), window // dilation, dilation, slopes)
    return merge(o), merge(l), merge(m)


def dilated_branch_sample(q, k_all, v_all, window, dilation, slopes):
    f32 = jnp.float32
    s_len = q.shape[1]
    t_len = k_all.shape[1]
    p_len = t_len - s_len
    dist = jnp.arange(window // dilation + 1) * dilation
    idx = p_len + jnp.arange(s_len)[:, None] - dist[None, :]
    valid = idx >= 0
    idx = jnp.clip(idx, 0, t_len - 1)
    kg = jnp.take(k_all, idx, axis=1).astype(f32)
    vg = jnp.take(v_all, idx, axis=1).astype(f32)
    s = jnp.einsum('bshd,bskhd->bhsk', q.astype(f32), kg) * (1.0 / math.sqrt(q.shape[-1]))
    s = s - slopes[None, :, None, None] * dist.astype(f32)[None, None, None, :]
    s = jnp.where(valid[None, None], s, -jnp.inf)
    m = jnp.max(s, -1)
    p = jnp.exp(s - m[..., None])
    l = jnp.sum(p, -1)
    o = jnp.einsum('bhsk,bskhd->bshd', p, vg)
    return o, jnp.transpose(l, (0, 2, 1)), jnp.transpose(m, (0, 2, 1))


def combine_branches(branches):
    m_max = branches[0][2]
    for _, _, m in branches[1:]:
        m_max = jnp.maximum(m_max, m)
    num = 0.0
    den = 0.0
    for o, l, m in branches:
        w = jnp.exp(m - m_max)
        num = num + w[..., None] * o
        den = den + w * l
    out = num / den[..., None]
    return out.reshape(out.shape[0], out.shape[1], ATTN_WIDTH)


def _linear_recurrence_op(e1, e2):
    a1, b1 = e1
    a2, b2 = e2
    return a1 * a2, a2 * b1 + b2


def ssm_mixer(u, h0, a_re, a_im, log_dt, b_re, b_im, c_re, c_im, d_skip, w_glu):
    f32 = jnp.float32
    bsz, L, _ = u.shape
    A = lax.complex(a_re.astype(f32), a_im.astype(f32))
    dt = jnp.exp(log_dt.astype(f32))[:, None]
    a_bar = jnp.exp(dt * A)
    b_mat = lax.complex(b_re.astype(f32), b_im.astype(f32))
    b_bar = ((a_bar - 1.0) / A)[..., None] * b_mat
    c_mat = lax.complex(c_re.astype(f32), c_im.astype(f32))
    uf = u.astype(f32).reshape(bsz, L, N_SSM_GROUPS, SSM_GROUP)
    bu = jnp.einsum('gnc,blgc->blgn', b_bar, uf.astype(jnp.complex64))
    bu = bu.at[:, 0].add(a_bar * h0)
    decay = jnp.broadcast_to(a_bar, bu.shape)
    _, h = lax.associative_scan(_linear_recurrence_op, (decay, bu), axis=1)
    y = jnp.real(jnp.einsum('gcn,blgn->blgc', c_mat, h)) + d_skip.astype(f32).reshape(N_SSM_GROUPS, SSM_GROUP) * uf
    y = jax.nn.gelu(y.reshape(bsz, L, SSM_WIDTH)).astype(u.dtype)
    z = jnp.einsum('blc,ce->ble', y, w_glu)
    val, gate = jnp.split(z, 2, axis=-1)
    return val * jax.nn.sigmoid(gate), h[:, -1]


def merge_project(attn, ssm, g_attn, g_ssm, w_out):
    mixed = jnp.concatenate([rms_norm(attn, g_attn), rms_norm(ssm, g_ssm)], axis=-1)
    return jnp.einsum('blc,cd->bld', mixed, w_out)


def moe_ffn(x, w_router, b_router, w_gate, w_up, w_down):
    f32 = jnp.float32
    logits = jnp.einsum('bld,de->ble', x.astype(f32), w_router.astype(f32))
    scores = jax.nn.softmax(logits, axis=-1)
    sel = scores + b_router.astype(f32)
    grp = sel.reshape(sel.shape[0], sel.shape[1], N_EXPERT_GROUPS, EXPERTS_PER_GROUP)
    grp_score = jnp.sum(lax.top_k(grp, TOP_K)[0], -1)
    best = jnp.argmax(grp_score, -1)
    in_group = (jnp.arange(N_EXPERTS) // EXPERTS_PER_GROUP) == best[..., None]
    _, idx = lax.top_k(jnp.where(in_group, sel, -jnp.inf), TOP_K)
    w = jnp.take_along_axis(scores, idx, -1)
    w = w / jnp.sum(w, -1, keepdims=True)
    gate = jnp.einsum('blk,blke->ble', w, jax.nn.one_hot(idx, N_EXPERTS, dtype=f32)).astype(x.dtype)
    hid = jax.nn.silu(jnp.einsum('bld,edf->blef', x, w_gate)) * jnp.einsum('bld,edf->blef', x, w_up)
    return jnp.einsum('blef,efd->bld', hid * gate[..., None], w_down)


def setup_inputs(seed: int = 0) -> dict:
    key = jax.random.key(seed)
    ks = jax.random.split(key, 32)
    f32 = jnp.float32
    nrm = lambda k, shape, scale: jax.random.normal(k, shape, f32) * scale
    cache_rows = min(MAX_WINDOW, PAST_LEN)
    a_im_base = math.pi * jnp.arange(SSM_STATE, dtype=f32)
    return {
        "x_prompt": nrm(ks[0], (BATCH, SEQ, D_MODEL), 1.0),
        "x_sample": nrm(ks[1], (DEC_BATCH, DEC_SEQ, D_MODEL), 1.0),
        "cache_k": nrm(ks[2], (DEPTH, DEC_BATCH, cache_rows, N_HEADS, HEAD_DIM), 1.0),
        "cache_v": nrm(ks[3], (DEPTH, DEC_BATCH, cache_rows, N_HEADS, HEAD_DIM), 1.0),
        "state_ssm_re": nrm(ks[4], (DEPTH, DEC_BATCH, N_SSM_GROUPS, SSM_STATE), 0.5),
        "state_ssm_im": nrm(ks[5], (DEPTH, DEC_BATCH, N_SSM_GROUPS, SSM_STATE), 0.5),
        "w_in": nrm(ks[6], (DEPTH, D_MODEL, IN_COLS), D_MODEL ** -0.5),
        "w_out": nrm(ks[7], (DEPTH, D_MODEL, D_MODEL), DEEPNORM_BETA * D_MODEL ** -0.5),
        "g_attn_out": 1.0 + nrm(ks[8], (DEPTH, ATTN_WIDTH), 0.02),
        "g_ssm_out": 1.0 + nrm(ks[9], (DEPTH, SSM_WIDTH), 0.02),
        "ssm_a_re": -0.5 + nrm(ks[10], (DEPTH, N_SSM_GROUPS, SSM_STATE), 0.01),
        "ssm_a_im": a_im_base + nrm(ks[11], (DEPTH, N_SSM_GROUPS, SSM_STATE), 0.01),
        "ssm_log_dt": jax.random.uniform(ks[12], (DEPTH, N_SSM_GROUPS), f32, math.log(1e-3), math.log(1e-1)),
        "ssm_b_re": nrm(ks[13], (DEPTH, N_SSM_GROUPS, SSM_STATE, SSM_GROUP), (2.0 * SSM_GROUP) ** -0.5),
        "ssm_b_im": nrm(ks[14], (DEPTH, N_SSM_GROUPS, SSM_STATE, SSM_GROUP), (2.0 * SSM_GROUP) ** -0.5),
        "ssm_c_re": nrm(ks[15], (DEPTH, N_SSM_GROUPS, SSM_GROUP, SSM_STATE), (2.0 * SSM_STATE) ** -0.5),
        "ssm_c_im": nrm(ks[16], (DEPTH, N_SSM_GROUPS, SSM_GROUP, SSM_STATE), (2.0 * SSM_STATE) ** -0.5),
        "ssm_d": nrm(ks[17], (DEPTH, SSM_WIDTH), 1.0),
        "w_glu": nrm(ks[18], (DEPTH, SSM_WIDTH, 2 * SSM_WIDTH), SSM_WIDTH ** -0.5),
        "ln1_g": 1.0 + nrm(ks[19], (DEPTH, D_MODEL), 0.02),
        "ln1_b": nrm(ks[20], (DEPTH, D_MODEL), 0.02),
        "ln2_g": 1.0 + nrm(ks[21], (DEPTH, D_MODEL), 0.02),
        "ln2_b": nrm(ks[22], (DEPTH, D_MODEL), 0.02),
        "w_router": nrm(ks[23], (D_MODEL, N_EXPERTS), D_MODEL ** -0.5),
        "b_router": nrm(ks[24], (N_EXPERTS,), 0.01),
        "w_gate": nrm(ks[25], (DEPTH, N_EXPERTS, D_MODEL, D_EXPERT), D_MODEL ** -0.5),
        "w_up": nrm(ks[26], (DEPTH, N_EXPERTS, D_MODEL, D_EXPERT), D_MODEL ** -0.5),
        "w_down": nrm(ks[27], (DEPTH, N_EXPERTS, D_EXPERT, D_MODEL), DEEPNORM_BETA * D_EXPERT ** -0.5),
    }


def reference(x_prompt, x_sample, cache_k, cache_v, state_ssm_re, state_ssm_im,
              w_in, w_out, g_attn_out, g_ssm_out, ssm_a_re, ssm_a_im, ssm_log_dt,
              ssm_b_re, ssm_b_im, ssm_c_re, ssm_c_im, ssm_d, w_glu,
              ln1_g, ln1_b, ln2_g, ln2_b, w_router, b_router, w_gate, w_up, w_down):
    slopes = alibi_slopes()
    seq = x_prompt.shape[1]
    prompt_rows = min(MAX_WINDOW, seq)
    xp, xs = x_prompt, x_sample
    kp_out, vp_out, srp_out, sip_out = [], [], [], []
    ks_out, vs_out, srs_out, sis_out = [], [], [], []
    for l in range(DEPTH):
        ssm_params = (ssm_a_re[l], ssm_a_im[l], ssm_log_dt[l], ssm_b_re[l], ssm_b_im[l],
                      ssm_c_re[l], ssm_c_im[l], ssm_d[l], w_glu[l])
        q, k, v, u = project_in(xp, w_in[l])
        attn = combine_branches([dilated_branch_prompt(q, k, v, w, d, slopes) for w, d in DILATED_BRANCHES])
        h0 = jnp.zeros((xp.shape[0], N_SSM_GROUPS, SSM_STATE), jnp.complex64)
        ssm, h_last = ssm_mixer(u, h0, *ssm_params)
        mix = merge_project(attn.astype(xp.dtype), ssm, g_attn_out[l], g_ssm_out[l], w_out[l])
        xp = layer_norm(DEEPNORM_ALPHA * xp + mix, ln1_g[l], ln1_b[l])
        xp = layer_norm(DEEPNORM_ALPHA * xp + moe_ffn(xp, w_router, b_router, w_gate[l], w_up[l], w_down[l]), ln2_g[l], ln2_b[l])
        kp_out.append(k[:, seq - prompt_rows:])
        vp_out.append(v[:, seq - prompt_rows:])
        srp_out.append(jnp.real(h_last))
        sip_out.append(jnp.imag(h_last))
        q, k, v, u = project_in(xs, w_in[l])
        k_all = jnp.concatenate([cache_k[l].astype(k.dtype), k], axis=1)
        v_all = jnp.concatenate([cache_v[l].astype(v.dtype), v], axis=1)
        attn = combine_branches([dilated_branch_sample(q, k_all, v_all, w, d, slopes) for w, d in DILATED_BRANCHES])
        h0 = lax.complex(state_ssm_re[l].astype(jnp.float32), state_ssm_im[l].astype(jnp.float32))
        ssm, h_last = ssm_mixer(u, h0, *ssm_params)
        mix = merge_project(attn.astype(xs.dtype), ssm, g_attn_out[l], g_ssm_out[l], w_out[l])
        xs = layer_norm(DEEPNORM_ALPHA * xs + mix, ln1_g[l], ln1_b[l])
        xs = layer_norm(DEEPNORM_ALPHA * xs + moe_ffn(xs, w_router, b_router, w_gate[l], w_up[l], w_down[l]), ln2_g[l], ln2_b[l])
        ks_out.append(k)
        vs_out.append(v)
        srs_out.append(jnp.real(h_last))
        sis_out.append(jnp.imag(h_last))
    return (xp, xs,
            jnp.stack(kp_out), jnp.stack(vp_out), jnp.stack(srp_out), jnp.stack(sip_out),
            jnp.stack(ks_out), jnp.stack(vs_out), jnp.stack(srs_out), jnp.stack(sis_out))
```

```python
import functools
import math

import jax
import jax.numpy as jnp
from jax import lax
from jax.experimental import pallas as pl
from jax.experimental.pallas import tpu as pltpu

F32 = jnp.float32
BF16 = jnp.bfloat16
I32 = jnp.int32

D_MODEL = 2048
SEQ = 8192
DEPTH = 2
DEC_BATCH = 128
DEC_SEQ = 4
PAST_LEN = 2048
ATTN_WIDTH = D_MODEL // 2
SSM_WIDTH = D_MODEL - ATTN_WIDTH
HEAD_DIM = 64
N_HEADS = ATTN_WIDTH // HEAD_DIM
DILATED_BRANCHES = ((128, 1), (512, 4), (2048, 16))
LOOKBACK = 128
SSM_GROUP = 16
N_SSM_GROUPS = SSM_WIDTH // SSM_GROUP
N_PAIRS = N_SSM_GROUPS // 2
SSM_STATE = 64
PAIR_STATE = 4 * SSM_STATE
IN_COLS = 3 * ATTN_WIDTH + SSM_WIDTH
N_EXPERTS = 16
N_EXPERT_GROUPS = 4
EXPERTS_PER_GROUP = N_EXPERTS // N_EXPERT_GROUPS
D_EXPERT = D_MODEL // 2
DEEPNORM_ALPHA = (2.0 * DEPTH) ** 0.25
NORM_EPS = 1e-5

T_PROMPT = SEQ
T_SAMPLE = DEC_BATCH * DEC_SEQ
T_ALL = T_PROMPT + T_SAMPLE
SSM_CHUNK = 16
N_CHUNKS = T_PROMPT // SSM_CHUNK

V7X_VMEM_BYTES = 64 * 1024 * 1024
VMEM_LIMIT = V7X_VMEM_BYTES - 8 * 1024 * 1024

ROW_TILE = 256
EXPERT_TILE = 256
N_SLOTS = 2 * T_ALL + N_EXPERTS * EXPERT_TILE
N_EXPERT_TILES = N_SLOTS // EXPERT_TILE

ALIBI_SLOPES = tuple(2.0 ** (-8.0 * (h + 1) / N_HEADS) for h in range(N_HEADS))


def _cparams(*semantics):
    return pltpu.CompilerParams(dimension_semantics=semantics, vmem_limit_bytes=VMEM_LIMIT)


def _mm_body(x_ref, w_ref, o_ref):
    o_ref[...] = jnp.dot(x_ref[...].astype(BF16), w_ref[...], preferred_element_type=F32)


def _matmul(x, w, *, tm, tn, name):
    m, k = x.shape
    n = w.shape[1]
    return pl.pallas_call(
        _mm_body,
        grid=(n // tn, m // tm),
        in_specs=[pl.BlockSpec((tm, k), lambda j, i: (i, 0)),
                  pl.BlockSpec((k, tn), lambda j, i: (0, j))],
        out_specs=pl.BlockSpec((tm, tn), lambda j, i: (i, j)),
        out_shape=jax.ShapeDtypeStruct((m, n), F32),
        compiler_params=_cparams("parallel", "parallel"),
        name=name,
    )(x, w)


def _band_body(q_ref, kp_ref, kc_ref, vp_ref, vc_ref, o_ref, lse_ref, k_scr, v_scr, *, dilation):
    ib = pl.program_id(1)
    blk = LOOKBACK
    k_scr[0:blk, :] = kp_ref[...].astype(BF16)
    k_scr[blk:2 * blk, :] = kc_ref[...].astype(BF16)
    v_scr[0:blk, :] = vp_ref[...].astype(BF16)
    v_scr[blk:2 * blk, :] = vc_ref[...].astype(BF16)
    qi = lax.broadcasted_iota(I32, (blk, 2 * blk), 0)
    kj = lax.broadcasted_iota(I32, (blk, 2 * blk), 1)
    dist = qi + blk - kj
    first_key = jnp.where(ib == 0, blk, 0)
    valid = (dist >= 0) & (dist <= LOOKBACK) & (kj >= first_key)
    real_dist = (dist * dilation).astype(F32)
    low_lanes = lax.broadcasted_iota(I32, (blk, 2 * HEAD_DIM), 1) < HEAD_DIM
    for hp in range(N_HEADS // 2):
        cols = slice(hp * 2 * HEAD_DIM, (hp + 1) * 2 * HEAD_DIM)
        q2 = q_ref[:, cols].astype(BF16)
        k2 = k_scr[:, cols]
        v2 = v_scr[:, cols]
        outs, lses = [], []
        for sub in range(2):
            h = 2 * hp + sub
            qm = jnp.where(low_lanes if sub == 0 else jnp.logical_not(low_lanes), q2, jnp.zeros_like(q2))
            s = lax.dot_general(qm, k2, (((1,), (1,)), ((), ())), preferred_element_type=F32)
            s = jnp.where(valid, s * (1.0 / math.sqrt(HEAD_DIM)) - ALIBI_SLOPES[h] * real_dist, -jnp.inf)
            m = jnp.max(s, axis=-1, keepdims=True)
            p = jnp.exp(s - m)
            l = jnp.sum(p, axis=-1, keepdims=True)
            o = jnp.dot(p.astype(BF16), v2, preferred_element_type=F32)
            outs.append(o / l)
            lses.append(jnp.broadcast_to(m + jnp.log(l), (blk, 2 * HEAD_DIM)))
        o_ref[:, cols] = jnp.where(low_lanes, outs[0], outs[1])
        lse_ref[:, cols] = jnp.where(low_lanes, lses[0], lses[1])


def _band_attention(z, dilation):
    n = T_PROMPT // dilation
    nb = n // LOOKBACK
    cb = IN_COLS // ATTN_WIDTH
    zv = z.reshape(T_ALL // dilation, dilation * IN_COLS)
    blk = (LOOKBACK, ATTN_WIDTH)
    o, lse = pl.pallas_call(
        functools.partial(_band_body, dilation=dilation),
        grid=(dilation, nb),
        in_specs=[pl.BlockSpec(blk, lambda r, i: (i, cb * r)),
                  pl.BlockSpec(blk, lambda r, i: (jnp.maximum(i - 1, 0), cb * r + 1)),
                  pl.BlockSpec(blk, lambda r, i: (i, cb * r + 1)),
                  pl.BlockSpec(blk, lambda r, i: (jnp.maximum(i - 1, 0), cb * r + 2)),
                  pl.BlockSpec(blk, lambda r, i: (i, cb * r + 2))],
        out_specs=[pl.BlockSpec(blk, lambda r, i: (i, r)),
                   pl.BlockSpec(blk, lambda r, i: (i, r))],
        out_shape=[jax.ShapeDtypeStruct((n, dilation * ATTN_WIDTH), F32)] * 2,
        scratch_shapes=[pltpu.VMEM((2 * LOOKBACK, ATTN_WIDTH), BF16)] * 2,
        compiler_params=_cparams("parallel", "arbitrary"),
        name=f"band_attention_d{dilation}",
    )(zv, zv, zv, zv, zv)
    return o.reshape(T_PROMPT, ATTN_WIDTH), lse.reshape(T_PROMPT, ATTN_WIDTH)


def _combine_body(o1, o2, o3, l1, l2, l3, out_ref):
    a, b, c = l1[...], l2[...], l3[...]
    mx = jnp.maximum(jnp.maximum(a, b), c)
    wa, wb, wc = jnp.exp(a - mx), jnp.exp(b - mx), jnp.exp(c - mx)
    out_ref[...] = (wa * o1[...] + wb * o2[...] + wc * o3[...]) / (wa + wb + wc)


def _combine_branches(branches):
    spec = pl.BlockSpec((ROW_TILE, ATTN_WIDTH), lambda i: (i, 0))
    return pl.pallas_call(
        _combine_body,
        grid=(T_PROMPT // ROW_TILE,),
        in_specs=[spec] * 6,
        out_specs=spec,
        out_shape=jax.ShapeDtypeStruct((T_PROMPT, ATTN_WIDTH), F32),
        compiler_params=_cparams("parallel"),
        name="combine_branches",
    )(*[b[0] for b in branches], *[b[1] for b in branches])


NEAR_ROWS = 512
NEAR_PAD = NEAR_ROWS + 128
FAR_DILATION = 16
FAR_ROWS = PAST_LEN // FAR_DILATION
SAMPLE_PAD = 8


def _sample_attn_body(zs_ref, kn_ref, kf_ref, vn_ref, vf_ref, o_ref, k_scr, v_scr):
    s_len = DEC_SEQ
    rows = s_len * N_HEADS
    zs = zs_ref[...]
    q = zs[:, 0:ATTN_WIDTH]
    k_new = zs[:, ATTN_WIDTH:2 * ATTN_WIDTH]
    v_new = zs[:, 2 * ATTN_WIDTH:3 * ATTN_WIDTH]
    pad = jnp.zeros((NEAR_PAD - NEAR_ROWS - SAMPLE_PAD, ATTN_WIDTH), F32)
    k_scr[0:NEAR_ROWS, :] = kn_ref[...].astype(BF16)
    k_scr[NEAR_ROWS:NEAR_PAD, :] = jnp.concatenate([k_new, pad], axis=0).astype(BF16)
    v_scr[0:NEAR_ROWS, :] = vn_ref[...].astype(BF16)
    v_scr[NEAR_ROWS:NEAR_PAD, :] = jnp.concatenate([v_new, pad], axis=0).astype(BF16)

    lane_head = jnp.right_shift(lax.broadcasted_iota(I32, (N_HEADS, ATTN_WIDTH), 1), HEAD_DIM.bit_length() - 1)
    head_mask = lane_head == lax.broadcasted_iota(I32, (N_HEADS, ATTN_WIDTH), 0)
    qbd = jnp.concatenate(
        [jnp.where(head_mask, jnp.broadcast_to(q[s:s + 1, :], (N_HEADS, ATTN_WIDTH)), 0.0)
         for s in range(s_len)], axis=0).astype(BF16)
    row_head = jnp.bitwise_and(lax.broadcasted_iota(I32, (rows, 1), 0), N_HEADS - 1)
    slope = jnp.exp2((row_head + 1).astype(F32) * (-8.0 / N_HEADS))
    scale = 1.0 / math.sqrt(HEAD_DIM)
    nt = (((1,), (1,)), ((), ()))

    sn = lax.dot_general(qbd, k_scr[...], nt, preferred_element_type=F32)
    row_s = jnp.right_shift(lax.broadcasted_iota(I32, (rows, NEAR_PAD), 0), N_HEADS.bit_length() - 1)
    col = lax.broadcasted_iota(I32, (rows, NEAR_PAD), 1)
    is_cache = col < NEAR_ROWS
    dist = jnp.where(is_cache, NEAR_ROWS + row_s - col, row_s - (col - NEAR_ROWS))
    ok = (dist >= 0) & (col < NEAR_ROWS + s_len)
    mult = ((ok & (dist <= DILATED_BRANCHES[0][0])).astype(F32)
            + (ok & (dist <= DILATED_BRANCHES[1][0]) & ((dist & 3) == 0)).astype(F32)
            + (ok & jnp.logical_not(is_cache) & (dist == 0)).astype(F32))
    sn = jnp.where(mult > 0, sn * scale - slope * dist.astype(F32), -jnp.inf)

    sf_parts = []
    for s in range(s_len):
        kf = kf_ref[:, s * ATTN_WIDTH:(s + 1) * ATTN_WIDTH].astype(BF16)
        sf_parts.append(lax.dot_general(qbd[s * N_HEADS:(s + 1) * N_HEADS, :], kf, nt,
                                        preferred_element_type=F32))
    sf = jnp.concatenate(sf_parts, axis=0)
    far_dist = (PAST_LEN - FAR_DILATION * lax.broadcasted_iota(I32, (rows, FAR_ROWS), 1)).astype(F32)
    sf = sf * scale - slope * far_dist

    m = jnp.maximum(jnp.max(sn, axis=-1, keepdims=True), jnp.max(sf, axis=-1, keepdims=True))
    pn = mult * jnp.exp(sn - m)
    pf = jnp.exp(sf - m)
    l = jnp.sum(pn, axis=-1, keepdims=True) + jnp.sum(pf, axis=-1, keepdims=True)
    acc = jnp.dot(pn.astype(BF16), v_scr[...], preferred_element_type=F32)
    far_acc = []
    for s in range(s_len):
        vf = vf_ref[:, s * ATTN_WIDTH:(s + 1) * ATTN_WIDTH].astype(BF16)
        far_acc.append(jnp.dot(pf[s * N_HEADS:(s + 1) * N_HEADS, :].astype(BF16), vf,
                               preferred_element_type=F32))
    acc = (acc + jnp.concatenate(far_acc, axis=0)) / l
    o_ref[...] = jnp.zeros_like(o_ref)
    for s in range(s_len):
        o_ref[s:s + 1, :] = jnp.sum(jnp.where(head_mask, acc[s * N_HEADS:(s + 1) * N_HEADS, :], 0.0),
                                    axis=0, keepdims=True)


def _sample_attention(z, cache_k, cache_v):
    zs = jnp.pad(z[T_PROMPT:].reshape(DEC_BATCH, DEC_SEQ, IN_COLS), ((0, 0), (0, SAMPLE_PAD - DEC_SEQ), (0, 0)))
    near = cache_k.reshape(DEC_BATCH, PAST_LEN, ATTN_WIDTH), cache_v.reshape(DEC_BATCH, PAST_LEN, ATTN_WIDTH)
    far = (cache_k.reshape(DEC_BATCH, FAR_ROWS, FAR_DILATION * ATTN_WIDTH),
           cache_v.reshape(DEC_BATCH, FAR_ROWS, FAR_DILATION * ATTN_WIDTH))
    near_spec = pl.BlockSpec((None, NEAR_ROWS, ATTN_WIDTH), lambda b: (b, PAST_LEN // NEAR_ROWS - 1, 0))
    far_spec = pl.BlockSpec((None, FAR_ROWS, DEC_SEQ * ATTN_WIDTH), lambda b: (b, 0, 0))
    out = pl.pallas_call(
        _sample_attn_body,
        grid=(DEC_BATCH,),
        in_specs=[pl.BlockSpec((None, SAMPLE_PAD, IN_COLS), lambda b: (b, 0, 0)),
                  near_spec, far_spec, near_spec, far_spec],
        out_specs=pl.BlockSpec((None, SAMPLE_PAD, ATTN_WIDTH), lambda b: (b, 0, 0)),
        out_shape=jax.ShapeDtypeStruct((DEC_BATCH, SAMPLE_PAD, ATTN_WIDTH), F32),
        scratch_shapes=[pltpu.VMEM((NEAR_PAD, ATTN_WIDTH), BF16)] * 2,
        compiler_params=_cparams("parallel"),
        name="sample_attention",
    )(zs, near[0], far[0], near[1], far[1])
    return out[:, :DEC_SEQ].reshape(T_SAMPLE, ATTN_WIDTH)


def _ssm_matrices(a_re, a_im, log_dt, b_re, b_im, c_re, c_im, chunk):
    hp = lax.Precision.HIGHEST
    dt = jnp.exp(log_dt.astype(F32))[:, None]
    a_re, a_im = a_re.astype(F32), a_im.astype(F32)
    ks = jnp.arange(chunk + 1, dtype=F32)[:, None, None]
    mag = jnp.exp(ks * dt * a_re)
    pw_re, pw_im = mag * jnp.cos(ks * dt * a_im), mag * jnp.sin(ks * dt * a_im)
    den = a_re * a_re + a_im * a_im
    n_re, n_im = pw_re[1] - 1.0, pw_im[1]
    f_re, f_im = (n_re * a_re + n_im * a_im) / den, (n_im * a_re - n_re * a_im) / den
    b_re, b_im = b_re.astype(F32), b_im.astype(F32)
    bb_re = f_re[..., None] * b_re - f_im[..., None] * b_im
    bb_im = f_re[..., None] * b_im + f_im[..., None] * b_re
    c_re, c_im = c_re.astype(F32), c_im.astype(F32)
    ca_re = c_re[None] * pw_re[:, :, None, :] - c_im[None] * pw_im[:, :, None, :]
    ca_im = c_re[None] * pw_im[:, :, None, :] + c_im[None] * pw_re[:, :, None, :]
    kern = (jnp.einsum('lgcn,gnd->lgcd', ca_re[:chunk], bb_re, precision=hp)
            - jnp.einsum('lgcn,gnd->lgcd', ca_im[:chunk], bb_im, precision=hp))
    eye = jnp.eye(2, dtype=F32)
    pair = lambda x, ax: x.reshape(x.shape[:ax] + (N_PAIRS, 2) + x.shape[ax + 1:])
    tc = chunk * 2 * SSM_GROUP

    s_idx = jnp.arange(chunk)
    lag = s_idx[None, :] - s_idx[:, None]
    wt = jnp.where((lag >= 0)[:, :, None, None, None], kern[jnp.clip(lag, 0)], 0.0)
    w = jnp.einsum('stjgcd,gh->jshdtgc', pair(wt, 2), eye).reshape(N_PAIRS, tc, tc)

    rev_re, rev_im = pw_re[chunk - 1::-1][:chunk], pw_im[chunk - 1::-1][:chunk]
    p_re = rev_re[..., None] * bb_re[None] - rev_im[..., None] * bb_im[None]
    p_im = rev_re[..., None] * bb_im[None] + rev_im[..., None] * bb_re[None]
    p = jnp.einsum('rsjgnd,gh->jshdrgn', pair(jnp.stack([p_re, p_im]), 2), eye)
    p = p.reshape(N_PAIRS, tc, PAIR_STATE)

    r = jnp.einsum('rtjgcn,gh->jrhntgc', pair(jnp.stack([ca_re[1:], -ca_im[1:]]), 2), eye)
    r = r.reshape(N_PAIRS, PAIR_STATE, tc)

    at_re = pair(pw_re[chunk], 0).reshape(N_PAIRS, 2 * SSM_STATE)
    at_im = pair(pw_im[chunk], 0).reshape(N_PAIRS, 2 * SSM_STATE)
    m1 = jnp.concatenate([at_re, at_re], axis=-1)
    m2 = jnp.concatenate([-at_im, at_im], axis=-1)
    return w.astype(BF16), p.astype(BF16), r.astype(BF16), m1, m2


def _swap_halves(h, width):
    parts = []
    for c in range(h.shape[-1] // (2 * width)):
        parts += [h[..., (2 * c + 1) * width:(2 * c + 2) * width], h[..., 2 * c * width:(2 * c + 1) * width]]
    return jnp.concatenate(parts, axis=-1)


def _ssm_z_body(u_ref, p_ref, z_ref):
    z_ref[...] = jnp.dot(u_ref[...].astype(BF16), p_ref[...], preferred_element_type=F32)


def _ssm_scan_body(z_ref, m1_ref, m2_ref, hin_ref, hlast_ref, h_scr):
    @pl.when(pl.program_id(0) == 0)
    def _():
        h_scr[...] = jnp.zeros_like(h_scr)

    m1, m2 = m1_ref[...], m2_ref[...]

    def step(k, h):
        hin_ref[k] = h
        return m1 * h + m2 * _swap_halves(h, 2 * SSM_STATE) + z_ref[k]

    h = lax.fori_loop(0, z_ref.shape[0], step, h_scr[...])
    h_scr[...] = h
    hlast_ref[...] = h


def _ssm_y_body(u_ref, w_ref, hin_ref, r_ref, y_ref):
    y_ref[...] = (jnp.dot(u_ref[...].astype(BF16), w_ref[...], preferred_element_type=F32)
                  + jnp.dot(hin_ref[...].astype(BF16), r_ref[...], preferred_element_type=F32))


def _ssm_prompt(u, mats):
    w, p, r, m1, m2 = mats
    tc = SSM_CHUNK * 2 * SSM_GROUP
    u_pairs = u.reshape(N_CHUNKS, SSM_CHUNK, N_PAIRS, 2 * SSM_GROUP).transpose(2, 0, 1, 3)
    u_pairs = u_pairs.reshape(N_PAIRS, N_CHUNKS, tc)
    sub = 8
    state_w = N_PAIRS * PAIR_STATE // sub
    z = pl.pallas_call(
        _ssm_z_body,
        grid=(N_PAIRS,),
        in_specs=[pl.BlockSpec((None, N_CHUNKS, tc), lambda j: (j, 0, 0)),
                  pl.BlockSpec((None, tc, PAIR_STATE), lambda j: (j, 0, 0))],
        out_specs=pl.BlockSpec((N_CHUNKS, PAIR_STATE), lambda j: (0, j)),
        out_shape=jax.ShapeDtypeStruct((N_CHUNKS, N_PAIRS * PAIR_STATE), F32),
        compiler_params=_cparams("parallel"),
        name="ssm_chunk_state",
    )(u_pairs, p)
    scan_rows = 128
    hin, hlast = pl.pallas_call(
        _ssm_scan_body,
        grid=(N_CHUNKS // scan_rows,),
        in_specs=[pl.BlockSpec((scan_rows, sub, state_w), lambda i: (i, 0, 0)),
                  pl.BlockSpec((sub, state_w), lambda i: (0, 0)),
                  pl.BlockSpec((sub, state_w), lambda i: (0, 0))],
        out_specs=[pl.BlockSpec((scan_rows, sub, state_w), lambda i: (i, 0, 0)),
                   pl.BlockSpec((sub, state_w), lambda i: (0, 0))],
        out_shape=[jax.ShapeDtypeStruct((N_CHUNKS, sub, state_w), F32),
                   jax.ShapeDtypeStruct((sub, state_w), F32)],
        scratch_shapes=[pltpu.VMEM((sub, state_w), F32)],
        compiler_params=_cparams("arbitrary"),
        name="ssm_scan",
    )(z.reshape(N_CHUNKS, sub, state_w), m1.reshape(sub, state_w), m2.reshape(sub, state_w))
    y = pl.pallas_call(
        _ssm_y_body,
        grid=(N_PAIRS,),
        in_specs=[pl.BlockSpec((None, N_CHUNKS, tc), lambda j: (j, 0, 0)),
                  pl.BlockSpec((None, tc, tc), lambda j: (j, 0, 0)),
                  pl.BlockSpec((N_CHUNKS, PAIR_STATE), lambda j: (0, j)),
                  pl.BlockSpec((None, PAIR_STATE, tc), lambda j: (j, 0, 0))],
        out_specs=pl.BlockSpec((None, N_CHUNKS, tc), lambda j: (j, 0, 0)),
        out_shape=jax.ShapeDtypeStruct((N_PAIRS, N_CHUNKS, tc), F32),
        compiler_params=_cparams("parallel"),
        name="ssm_chunk_output",
    )(u_pairs, w, hin.reshape(N_CHUNKS, N_PAIRS * PAIR_STATE), r)
    y = y.reshape(N_PAIRS, N_CHUNKS, SSM_CHUNK, 2 * SSM_GROUP).transpose(1, 2, 0, 3)
    return y.reshape(T_PROMPT, SSM_WIDTH), hlast.reshape(N_PAIRS, PAIR_STATE)


def _ssm_sample_body(u_ref, h0_ref, p_ref, w_ref, r_ref, m1_ref, m2_ref, y_ref, hl_ref):
    u = u_ref[...].astype(BF16)
    h0 = h0_ref[...]
    y_ref[...] = (jnp.dot(u, w_ref[...], preferred_element_type=F32)
                  + jnp.dot(h0.astype(BF16), r_ref[...], preferred_element_type=F32))
    hl_ref[...] = (m1_ref[...] * h0 + m2_ref[...] * _swap_halves(h0, 2 * SSM_STATE)
                   + jnp.dot(u, p_ref[...], preferred_element_type=F32))


def _ssm_sample(u, h0, mats):
    w, p, r, m1, m2 = mats
    tc = DEC_SEQ * 2 * SSM_GROUP
    u_pairs = u.reshape(DEC_BATCH, DEC_SEQ, N_PAIRS, 2 * SSM_GROUP).transpose(2, 0, 1, 3)
    u_pairs = u_pairs.reshape(N_PAIRS, DEC_BATCH, tc)
    per_pair = lambda *shape: pl.BlockSpec((None,) + shape, lambda j: (j,) + (0,) * len(shape))
    y, hl = pl.pallas_call(
        _ssm_sample_body,
        grid=(N_PAIRS,),
        in_specs=[per_pair(DEC_BATCH, tc), per_pair(DEC_BATCH, PAIR_STATE), per_pair(tc, PAIR_STATE),
                  per_pair(tc, tc), per_pair(PAIR_STATE, tc), per_pair(1, PAIR_STATE), per_pair(1, PAIR_STATE)],
        out_specs=[per_pair(DEC_BATCH, tc), per_pair(DEC_BATCH, PAIR_STATE)],
        out_shape=[jax.ShapeDtypeStruct((N_PAIRS, DEC_BATCH, tc), F32),
                   jax.ShapeDtypeStruct((N_PAIRS, DEC_BATCH, PAIR_STATE), F32)],
        compiler_params=_cparams("parallel"),
        name="ssm_sample",
    )(u_pairs, h0, p, w, r, m1[:, None, :], m2[:, None, :])
    y = y.reshape(N_PAIRS, DEC_BATCH, DEC_SEQ, 2 * SSM_GROUP).transpose(1, 2, 0, 3)
    return y.reshape(T_SAMPLE, SSM_WIDTH), hl


def _state_to_pairs(re, im):
    b = re.shape[0]
    f = lambda x: x.astype(F32).reshape(b, N_PAIRS, 2 * SSM_STATE).transpose(1, 0, 2)
    return jnp.concatenate([f(re), f(im)], axis=-1)


def _pairs_to_state(h):
    b = h.shape[1]
    f = lambda x: x.transpose(1, 0, 2).reshape(b, N_SSM_GROUPS, SSM_STATE)
    return f(h[..., :2 * SSM_STATE]), f(h[..., 2 * SSM_STATE:])


def _glu_body(y_ref, u_ref, d_ref, w_ref, o_ref):
    y = jax.nn.gelu(y_ref[...] + d_ref[...] * u_ref[...])
    zz = jnp.dot(y.astype(BF16), w_ref[...], preferred_element_type=F32)
    val, gate = zz[:, :SSM_WIDTH], zz[:, SSM_WIDTH:]
    o_ref[...] = val / (1.0 + jnp.exp(-gate))


def _glu(y, z, d_skip, w_glu):
    tm = 2 * ROW_TILE
    return pl.pallas_call(
        _glu_body,
        grid=(T_ALL // tm,),
        in_specs=[pl.BlockSpec((tm, SSM_WIDTH), lambda i: (i, 0)),
                  pl.BlockSpec((tm, SSM_WIDTH), lambda i: (i, 3 * ATTN_WIDTH // SSM_WIDTH)),
                  pl.BlockSpec((1, SSM_WIDTH), lambda i: (0, 0)),
                  pl.BlockSpec((SSM_WIDTH, 2 * SSM_WIDTH), lambda i: (0, 0))],
        out_specs=pl.BlockSpec((tm, SSM_WIDTH), lambda i: (i, 0)),
        out_shape=jax.ShapeDtypeStruct((T_ALL, SSM_WIDTH), F32),
        compiler_params=_cparams("parallel"),
        name="ssm_glu",
    )(y, z, d_skip.astype(F32).reshape(1, SSM_WIDTH), w_glu)


def _layer_norm(x, g, b):
    mu = jnp.mean(x, axis=-1, keepdims=True)
    xc = x - mu
    var = jnp.mean(xc * xc, axis=-1, keepdims=True)
    return xc * lax.rsqrt(var + NORM_EPS) * g + b


def _rms_norm(x, g):
    return x * lax.rsqrt(jnp.mean(x * x, axis=-1, keepdims=True) + NORM_EPS) * g


def _merge_body(ap_ref, as_ref, ssm_ref, x_ref, ga_ref, gs_ref, wa_ref, ws_ref, g_ref, b_ref, o_ref):
    is_prompt = pl.program_id(0) < T_PROMPT // ROW_TILE
    attn = jnp.where(is_prompt, ap_ref[...], as_ref[...])
    an = _rms_norm(attn, ga_ref[...]).astype(BF16)
    sn = _rms_norm(ssm_ref[...], gs_ref[...]).astype(BF16)
    mix = (jnp.dot(an, wa_ref[...], preferred_element_type=F32)
           + jnp.dot(sn, ws_ref[...], preferred_element_type=F32))
    o_ref[...] = _layer_norm(DEEPNORM_ALPHA * x_ref[...] + mix, g_ref[...], b_ref[...])


def _merge(attn_p, attn_s, ssm, x, g_attn, g_ssm, w_out, ln_g, ln_b):
    n_p = T_PROMPT // ROW_TILE
    row = lambda width: pl.BlockSpec((ROW_TILE, width), lambda i: (i, 0))
    vec = lambda width: pl.BlockSpec((1, width), lambda i: (0, 0))
    half = pl.BlockSpec((ATTN_WIDTH, D_MODEL), lambda i: (0, 0))
    as_row = lambda v: v.astype(F32).reshape(1, -1)
    return pl.pallas_call(
        _merge_body,
        grid=(T_ALL // ROW_TILE,),
        in_specs=[pl.BlockSpec((ROW_TILE, ATTN_WIDTH), lambda i: (jnp.minimum(i, n_p - 1), 0)),
                  pl.BlockSpec((ROW_TILE, ATTN_WIDTH), lambda i: (jnp.maximum(i - n_p, 0), 0)),
                  row(SSM_WIDTH), row(D_MODEL), vec(ATTN_WIDTH), vec(SSM_WIDTH), half, half,
                  vec(D_MODEL), vec(D_MODEL)],
        out_specs=row(D_MODEL),
        out_shape=jax.ShapeDtypeStruct((T_ALL, D_MODEL), F32),
        compiler_params=_cparams("parallel"),
        name="merge_project_ln1",
    )(attn_p, attn_s, ssm, x, as_row(g_attn), as_row(g_ssm), w_out[:ATTN_WIDTH], w_out[ATTN_WIDTH:],
      as_row(ln_g), as_row(ln_b))


ROUTER_TILE = 512
INFO_ROWS = 8


def _top2_of4(a, b, c, d):
    m1, n1 = jnp.maximum(a, b), jnp.minimum(a, b)
    m2, n2 = jnp.maximum(c, d), jnp.minimum(c, d)
    return jnp.maximum(m1, m2) + jnp.maximum(jnp.minimum(m1, m2), jnp.maximum(n1, n2))


def _router_body(x_ref, wr_ref, br_ref, tri_ref, info_ref, cnt_ref, carry):
    @pl.when(pl.program_id(0) == 0)
    def _():
        carry[...] = jnp.zeros_like(carry)

    tm = x_ref.shape[0]
    logits = lax.dot_general(wr_ref[...], x_ref[...], (((1,), (1,)), ((), ())),
                             precision=lax.Precision.HIGHEST, preferred_element_type=F32)
    ex = jnp.exp(logits - jnp.max(logits, axis=0, keepdims=True))
    scores = ex / jnp.sum(ex, axis=0, keepdims=True)
    sel = scores + br_ref[...]
    row = lambda a, i: a[i:i + 1, :]
    epg = EXPERTS_PER_GROUP
    grp = [_top2_of4(*[row(sel, epg * g + i) for i in range(epg)]) for g in range(N_EXPERT_GROUPS)]
    best, best_v = jnp.zeros((1, tm), I32), grp[0]
    for g in range(1, N_EXPERT_GROUPS):
        upd = grp[g] > best_v
        best, best_v = jnp.where(upd, g, best), jnp.where(upd, grp[g], best_v)

    def in_group(a, i):
        out = row(a, i)
        for g in range(1, N_EXPERT_GROUPS):
            out = jnp.where(best == g, row(a, epg * g + i), out)
        return out

    v = [in_group(sel, i) for i in range(epg)]
    sc = [in_group(scores, i) for i in range(epg)]
    i0, v0, s0 = jnp.zeros((1, tm), I32), v[0], sc[0]
    for i in range(1, epg):
        upd = v[i] > v0
        i0, v0, s0 = jnp.where(upd, i, i0), jnp.where(upd, v[i], v0), jnp.where(upd, sc[i], s0)
    i1, v1, s1 = jnp.zeros((1, tm), I32), jnp.full((1, tm), -jnp.inf, F32), jnp.zeros((1, tm), F32)
    for i in range(epg):
        upd = (i0 != i) & (v[i] > v1)
        i1, v1, s1 = jnp.where(upd, i, i1), jnp.where(upd, v[i], v1), jnp.where(upd, sc[i], s1)
    e0, e1 = best * epg + i0, best * epg + i1
    wsum = s0 + s1

    eidx = lax.broadcasted_iota(I32, (N_EXPERTS, tm), 0)
    hit0, hit1 = eidx == e0, eidx == e1
    onehot = jnp.where(hit0 | hit1, 1.0, 0.0)
    before = jnp.dot(onehot.astype(BF16), tri_ref[...], preferred_element_type=F32) + carry[...]
    rank0 = jnp.sum(jnp.where(hit0, before, 0.0), axis=0, keepdims=True)
    rank1 = jnp.sum(jnp.where(hit1, before, 0.0), axis=0, keepdims=True)
    total = carry[...] + jnp.sum(onehot, axis=1, keepdims=True)
    carry[...] = total
    cnt_ref[...] = jnp.broadcast_to(total, cnt_ref.shape)
    info_ref[...] = jnp.zeros_like(info_ref)
    for i, val in enumerate((e0.astype(F32), e1.astype(F32), s0 / wsum, s1 / wsum, rank0, rank1)):
        info_ref[i:i + 1, :] = val


def _router(x, w_router, b_router):
    tm = ROUTER_TILE
    tri = (jnp.arange(tm)[:, None] < jnp.arange(tm)[None, :]).astype(BF16)
    return pl.pallas_call(
        _router_body,
        grid=(T_ALL // tm,),
        in_specs=[pl.BlockSpec((tm, D_MODEL), lambda i: (i, 0)),
                  pl.BlockSpec((N_EXPERTS, D_MODEL), lambda i: (0, 0)),
                  pl.BlockSpec((N_EXPERTS, 1), lambda i: (0, 0)),
                  pl.BlockSpec((tm, tm), lambda i: (0, 0))],
        out_specs=[pl.BlockSpec((INFO_ROWS, tm), lambda i: (0, i)),
                   pl.BlockSpec((N_EXPERTS, 128), lambda i: (0, 0))],
        out_shape=[jax.ShapeDtypeStruct((INFO_ROWS, T_ALL), F32),
                   jax.ShapeDtypeStruct((N_EXPERTS, 128), F32)],
        scratch_shapes=[pltpu.VMEM((N_EXPERTS, 1), F32)],
        compiler_params=_cparams("arbitrary"),
        name="router",
    )(x, w_router.astype(F32).T, b_router.astype(F32).reshape(N_EXPERTS, 1), tri)


def _routing_tables(info, counts):
    cnt = counts[:, 0].astype(I32)
    tiles = (cnt + EXPERT_TILE - 1) // EXPERT_TILE
    tiles_end = jnp.cumsum(tiles)
    first_slot = (tiles_end - tiles) * EXPERT_TILE
    e0, e1 = info[0].astype(I32), info[1].astype(I32)
    dest0 = first_slot[e0] + info[4].astype(I32)
    dest1 = first_slot[e1] + info[5].astype(I32)
    n_valid = tiles_end[-1]
    tile = jnp.minimum(jnp.arange(N_EXPERT_TILES, dtype=I32), n_valid - 1)
    tile_expert = jnp.minimum(jnp.searchsorted(tiles_end, tile, side='right'), N_EXPERTS - 1).astype(I32)
    last_tile = jnp.where(tiles > 0, tiles_end - 1, -1).astype(I32)
    return dest0, dest1, tile, tile_expert, n_valid.reshape(1).astype(I32), last_tile


def _row_copy(src_ref, src_row, dst_ref, dst_row, sem):
    return pltpu.make_async_copy(src_ref.at[pl.ds(src_row, 1)], dst_ref.at[pl.ds(dst_row, 1)], sem)


def _dispatch_body(dest0, dest1, last_tile, n_valid, x_ref, xs_ref, zero_scr, sem):
    step = pl.program_id(0)
    tm = x_ref.shape[0]

    @pl.when(step == 0)
    def _():
        zero_scr[...] = jnp.zeros_like(zero_scr)
        fill = lambda t: pltpu.make_async_copy(zero_scr, xs_ref.at[pl.ds(t * EXPERT_TILE, EXPERT_TILE)], sem)
        padded = [(last_tile[e], last_tile[e] >= 0) for e in range(N_EXPERTS)]
        unused = [(N_EXPERT_TILES - 1 - j, N_EXPERT_TILES - 1 - j >= n_valid[0]) for j in range(N_EXPERTS)]
        for t, needed in padded + unused:
            @pl.when(needed)
            def _():
                fill(t).start()
        for t, needed in padded + unused:
            @pl.when(needed)
            def _():
                fill(t).wait()

    base = step * tm

    def start(r, c):
        _row_copy(x_ref, r, xs_ref, dest0[base + r], sem).start()
        _row_copy(x_ref, r, xs_ref, dest1[base + r], sem).start()
        return c

    def wait(r, c):
        _row_copy(x_ref, r, xs_ref, dest0[base + r], sem).wait()
        _row_copy(x_ref, r, xs_ref, dest1[base + r], sem).wait()
        return c

    lax.fori_loop(0, tm, start, 0)
    lax.fori_loop(0, tm, wait, 0)


def _dispatch(x, dest0, dest1, last_tile, n_valid):
    return pl.pallas_call(
        _dispatch_body,
        grid_spec=pltpu.PrefetchScalarGridSpec(
            num_scalar_prefetch=4,
            grid=(T_ALL // ROW_TILE,),
            in_specs=[pl.BlockSpec((ROW_TILE, D_MODEL), lambda i, *_: (i, 0))],
            out_specs=pl.BlockSpec(memory_space=pl.ANY),
            scratch_shapes=[pltpu.VMEM((EXPERT_TILE, D_MODEL), F32), pltpu.SemaphoreType.DMA],
        ),
        out_shape=jax.ShapeDtypeStruct((N_SLOTS, D_MODEL), F32),
        compiler_params=_cparams("arbitrary"),
        name="moe_dispatch",
    )(dest0, dest1, last_tile, n_valid, x)


def _expert_body(tile, tile_expert, n_valid, x_ref, wg_ref, wu_ref, wd_ref, y_ref):
    used = pl.program_id(0) < n_valid[0]

    @pl.when(used)
    def _():
        x = x_ref[...].astype(BF16)
        g = jnp.dot(x, wg_ref[...], preferred_element_type=F32)
        u = jnp.dot(x, wu_ref[...], preferred_element_type=F32)
        hid = (g / (1.0 + jnp.exp(-g))) * u
        y_ref[...] = jnp.dot(hid.astype(BF16), wd_ref[...], preferred_element_type=F32)

    @pl.when(jnp.logical_not(used))
    def _():
        y_ref[...] = jnp.zeros_like(y_ref)


def _experts(xs, tile, tile_expert, n_valid, w_gate, w_up, w_down):
    rows = pl.BlockSpec((EXPERT_TILE, D_MODEL), lambda i, tile, te, nv: (tile[i], 0))
    out_rows = pl.BlockSpec((EXPERT_TILE, D_MODEL), lambda i, tile, te, nv: (i, 0))
    return pl.pallas_call(
        _expert_body,
        grid_spec=pltpu.PrefetchScalarGridSpec(
            num_scalar_prefetch=3,
            grid=(N_EXPERT_TILES,),
            in_specs=[rows,
                      pl.BlockSpec((None, D_MODEL, D_EXPERT), lambda i, tile, te, nv: (te[i], 0, 0)),
                      pl.BlockSpec((None, D_MODEL, D_EXPERT), lambda i, tile, te, nv: (te[i], 0, 0)),
                      pl.BlockSpec((None, D_EXPERT, D_MODEL), lambda i, tile, te, nv: (te[i], 0, 0))],
            out_specs=out_rows,
        ),
        out_shape=jax.ShapeDtypeStruct((N_SLOTS, D_MODEL), F32),
        compiler_params=_cparams("arbitrary"),
        name="moe_experts",
    )(tile, tile_expert, n_valid, xs, w_gate, w_up, w_down)


def _combine_ln_body(dest0, dest1, ys_ref, x_ref, w0_ref, w1_ref, g_ref, b_ref, o_ref, buf0, buf1, sem):
    tm = x_ref.shape[0]
    base = pl.program_id(0) * tm

    def start(r, c):
        _row_copy(ys_ref, dest0[base + r], buf0, r, sem).start()
        _row_copy(ys_ref, dest1[base + r], buf1, r, sem).start()
        return c

    def wait(r, c):
        _row_copy(ys_ref, dest0[base + r], buf0, r, sem).wait()
        _row_copy(ys_ref, dest1[base + r], buf1, r, sem).wait()
        return c

    lax.fori_loop(0, tm, start, 0)
    lax.fori_loop(0, tm, wait, 0)
    y = w0_ref[...] * buf0[...] + w1_ref[...] * buf1[...]
    o_ref[...] = _layer_norm(DEEPNORM_ALPHA * x_ref[...] + y, g_ref[...], b_ref[...])


def _combine_ln(ys, x, dest0, dest1, w0, w1, ln_g, ln_b):
    row = lambda width: pl.BlockSpec((ROW_TILE, width), lambda i, *_: (i, 0))
    vec = pl.BlockSpec((1, D_MODEL), lambda i, *_: (0, 0))
    return pl.pallas_call(
        _combine_ln_body,
        grid_spec=pltpu.PrefetchScalarGridSpec(
            num_scalar_prefetch=2,
            grid=(T_ALL // ROW_TILE,),
            in_specs=[pl.BlockSpec(memory_space=pl.ANY), row(D_MODEL), row(1), row(1), vec, vec],
            out_specs=row(D_MODEL),
            scratch_shapes=[pltpu.VMEM((ROW_TILE, D_MODEL), F32)] * 2 + [pltpu.SemaphoreType.DMA],
        ),
        out_shape=jax.ShapeDtypeStruct((T_ALL, D_MODEL), F32),
        compiler_params=_cparams("arbitrary"),
        name="moe_combine_ln2",
    )(dest0, dest1, ys, x, w0.reshape(T_ALL, 1), w1.reshape(T_ALL, 1),
      ln_g.astype(F32).reshape(1, D_MODEL), ln_b.astype(F32).reshape(1, D_MODEL))


def _moe_ln(x, w_router, b_router, w_gate, w_up, w_down, ln_g, ln_b):
    info, counts = _router(x, w_router, b_router)
    dest0, dest1, tile, tile_expert, n_valid, last_tile = _routing_tables(info, counts)
    xs = _dispatch(x, dest0, dest1, last_tile, n_valid)
    ys = _experts(xs, tile, tile_expert, n_valid, w_gate, w_up, w_down)
    return _combine_ln(ys, x, dest0, dest1, info[2], info[3], ln_g, ln_b)


def kernel(x_prompt, x_sample, cache_k, cache_v, state_ssm_re, state_ssm_im, w_in, w_out, g_attn_out, g_ssm_out, ssm_a_re, ssm_a_im, ssm_log_dt, ssm_b_re, ssm_b_im, ssm_c_re, ssm_c_im, ssm_d, w_glu, ln1_g, ln1_b, ln2_g, ln2_b, w_router, b_router, w_gate, w_up, w_down):
    x = jnp.concatenate([x_prompt.reshape(T_PROMPT, D_MODEL), x_sample.reshape(T_SAMPLE, D_MODEL)], axis=0)
    x = x.astype(F32)
    kp_out, vp_out, srp_out, sip_out = [], [], [], []
    ks_out, vs_out, srs_out, sis_out = [], [], [], []
    for l in range(DEPTH):
        z = _matmul(x, w_in[l].astype(BF16), tm=2 * ROW_TILE, tn=ATTN_WIDTH, name="project_in")
        k_cols, v_cols = z[:, ATTN_WIDTH:2 * ATTN_WIDTH], z[:, 2 * ATTN_WIDTH:3 * ATTN_WIDTH]
        u = z[:, 3 * ATTN_WIDTH:]

        attn_p = _combine_branches([_band_attention(z, d) for _, d in DILATED_BRANCHES])
        attn_s = _sample_attention(z, cache_k[l], cache_v[l])

        ssm_params = (ssm_a_re[l], ssm_a_im[l], ssm_log_dt[l], ssm_b_re[l], ssm_b_im[l], ssm_c_re[l], ssm_c_im[l])
        y_p, h_p = _ssm_prompt(u[:T_PROMPT], _ssm_matrices(*ssm_params, SSM_CHUNK))
        y_s, h_s = _ssm_sample(u[T_PROMPT:], _state_to_pairs(state_ssm_re[l], state_ssm_im[l]),
                               _ssm_matrices(*ssm_params, DEC_SEQ))
        ssm = _glu(jnp.concatenate([y_p, y_s], axis=0), z, ssm_d[l], w_glu[l].astype(BF16))

        x = _merge(attn_p, attn_s, ssm, x, g_attn_out[l], g_ssm_out[l], w_out[l].astype(BF16),
                   ln1_g[l], ln1_b[l])
        x = _moe_ln(x, w_router, b_router, w_gate[l].astype(BF16), w_up[l].astype(BF16),
                    w_down[l].astype(BF16), ln2_g[l], ln2_b[l])

        heads = lambda a, b: a.reshape(b, -1, N_HEADS, HEAD_DIM)
        window = min(2048, SEQ)
        kp_out.append(heads(k_cols[T_PROMPT - window:T_PROMPT], 1))
        vp_out.append(heads(v_cols[T_PROMPT - window:T_PROMPT], 1))
        ks_out.append(heads(k_cols[T_PROMPT:], DEC_BATCH))
        vs_out.append(heads(v_cols[T_PROMPT:], DEC_BATCH))
        re_p, im_p = _pairs_to_state(h_p[:, None, :])
        re_s, im_s = _pairs_to_state(h_s)
        srp_out.append(re_p)
        sip_out.append(im_p)
        srs_out.append(re_s)
        sis_out.append(im_s)
    return (x[:T_PROMPT].reshape(1, SEQ, D_MODEL), x[T_PROMPT:].reshape(DEC_BATCH, DEC_SEQ, D_MODEL),
            jnp.stack(kp_out), jnp.stack(vp_out), jnp.stack(srp_out), jnp.stack(sip_out),
            jnp.stack(ks_out), jnp.stack(vs_out), jnp.stack(srs_out), jnp.stack(sis_out))
```

```python
import math

import jax
import jax.numpy as jnp
from jax import lax
from jax.experimental import pallas as pl
from jax.experimental.pallas import tpu as pltpu

F32 = jnp.float32
BF16 = jnp.bfloat16
I32 = jnp.int32

D_MODEL = 2048
SEQ = 8192
DEPTH = 2
DEC_BATCH = 128
DEC_SEQ = 4
PAST_LEN = 2048
ATTN_WIDTH = D_MODEL // 2
SSM_WIDTH = D_MODEL - ATTN_WIDTH
HEAD_DIM = 64
N_HEADS = ATTN_WIDTH // HEAD_DIM
DILATED_BRANCHES = ((128, 1), (512, 4), (2048, 16))
LOOKBACK = 128
SSM_GROUP = 16
N_SSM_GROUPS = SSM_WIDTH // SSM_GROUP
N_PAIRS = N_SSM_GROUPS // 2
SSM_STATE = 64
PAIR_STATE = 4 * SSM_STATE
IN_COLS = 3 * ATTN_WIDTH + SSM_WIDTH
N_EXPERTS = 16
N_EXPERT_GROUPS = 4
EXPERTS_PER_GROUP = N_EXPERTS // N_EXPERT_GROUPS
D_EXPERT = D_MODEL // 2
DEEPNORM_ALPHA = (2.0 * DEPTH) ** 0.25
NORM_EPS = 1e-5

T_PROMPT = SEQ
T_SAMPLE = DEC_BATCH * DEC_SEQ
T_ALL = T_PROMPT + T_SAMPLE
SSM_CHUNK = 16
N_CHUNKS = T_PROMPT // SSM_CHUNK

V7X_VMEM_BYTES = 64 * 1024 * 1024
VMEM_LIMIT = V7X_VMEM_BYTES - 8 * 1024 * 1024

ROW_TILE = 256
EXPERT_TILE = 256
N_SLOTS = 2 * T_ALL + N_EXPERTS * EXPERT_TILE
N_EXPERT_TILES = N_SLOTS // EXPERT_TILE

ATTN_SCALE = 1.0 / math.sqrt(HEAD_DIM)
SLOPE_LOG2_STEP = -8.0 / N_HEADS


def _cparams(*semantics):
    return pltpu.CompilerParams(dimension_semantics=semantics, vmem_limit_bytes=VMEM_LIMIT)


def _mm_body(x_ref, w_ref, o_ref):
    o_ref[...] = jnp.dot(x_ref[...].astype(BF16), w_ref[...], preferred_element_type=F32)


def _matmul(x, w, layer, *, tm, tn, name):
    m, k = x.shape
    n = w.shape[-1]
    return pl.pallas_call(
        _mm_body,
        grid=(n // tn, m // tm),
        in_specs=[pl.BlockSpec((tm, k), lambda j, i: (i, 0)),
                  pl.BlockSpec((None, k, tn), lambda j, i: (layer, 0, j))],
        out_specs=pl.BlockSpec((tm, tn), lambda j, i: (i, j)),
        out_shape=jax.ShapeDtypeStruct((m, n), F32),
        compiler_params=_cparams("parallel", "parallel"),
        name=name,
    )(x, w)


SUPER_BLOCK = 2048
PAIR_WIDTH = 2 * HEAD_DIM


def _prompt_attn_body(q_ref, kp_ref, kc_ref, vp_ref, vc_ref, o_ref, k_all, v_all, o_br, lse_br):
    sb = pl.program_id(0)
    hp = pl.program_id(1)
    blk = LOOKBACK
    k_all[0:SUPER_BLOCK, :] = kp_ref[...]
    k_all[SUPER_BLOCK:2 * SUPER_BLOCK, :] = kc_ref[...]
    v_all[0:SUPER_BLOCK, :] = vp_ref[...]
    v_all[SUPER_BLOCK:2 * SUPER_BLOCK, :] = vc_ref[...]
    qi = lax.broadcasted_iota(I32, (blk, 2 * blk), 0)
    kj = lax.broadcasted_iota(I32, (blk, 2 * blk), 1)
    dist = qi + blk - kj
    in_band = (dist >= 0) & (dist <= LOOKBACK)
    low_lanes = lax.broadcasted_iota(I32, (blk, PAIR_WIDTH), 1) < HEAD_DIM
    head_lanes = (low_lanes, jnp.logical_not(low_lanes))
    slopes = [jnp.exp2(jnp.full((blk, 1), SLOPE_LOG2_STEP, F32) * (2 * hp + sub + 1).astype(F32))
              for sub in range(2)]
    nt = (((1,), (1,)), ((), ()))

    for bi, (_, dil) in enumerate(DILATED_BRANCHES):
        per_residue = SUPER_BLOCK // (blk * dil)
        bias = [slopes[sub] * (dist * dil).astype(F32) for sub in range(2)]
        stride = dil if dil > 1 else None

        def block(idx, carry, bi=bi, dil=dil, per_residue=per_residue, bias=bias, stride=stride):
            g = jnp.bitwise_and(idx, per_residue - 1)
            r = jnp.right_shift(idx, per_residue.bit_length() - 1)
            q0 = dil * blk * g + r
            kv0 = SUPER_BLOCK - dil * blk + q0
            if dil == 1:
                q0, kv0 = pl.multiple_of(q0, blk), pl.multiple_of(kv0, blk)
            q_rows = pl.ds(q0, blk, stride=stride)
            kv_rows = pl.ds(kv0, 2 * blk, stride=stride)
            first_key = jnp.where((sb == 0) & (g == 0), blk, 0)
            valid = in_band & (kj >= first_key)
            q2 = q_ref[q_rows, :].astype(BF16)
            k2 = k_all[kv_rows, :].astype(BF16)
            v2 = v_all[kv_rows, :].astype(BF16)
            outs, lses = [], []
            for sub in range(2):
                qm = jnp.where(head_lanes[sub], q2, jnp.zeros_like(q2))
                s = lax.dot_general(qm, k2, nt, preferred_element_type=F32)
                s = jnp.where(valid, s * ATTN_SCALE - bias[sub], -jnp.inf)
                m = jnp.max(s, axis=-1, keepdims=True)
                p = jnp.exp(s - m)
                l = jnp.sum(p, axis=-1, keepdims=True)
                o = jnp.dot(p.astype(BF16), v2, preferred_element_type=F32)
                outs.append(o / l)
                lses.append(jnp.broadcast_to(m + jnp.log(l), (blk, PAIR_WIDTH)))
            o_br[bi, q_rows, :] = jnp.where(low_lanes, outs[0], outs[1])
            lse_br[bi, q_rows, :] = jnp.where(low_lanes, lses[0], lses[1])
            return carry

        lax.fori_loop(0, dil * per_residue, block, 0)

    a, b, c = lse_br[0], lse_br[1], lse_br[2]
    mx = jnp.maximum(jnp.maximum(a, b), c)
    wa, wb, wc = jnp.exp(a - mx), jnp.exp(b - mx), jnp.exp(c - mx)
    o_ref[...] = (wa * o_br[0] + wb * o_br[1] + wc * o_br[2]) / (wa + wb + wc)


def _prompt_attention(z):
    n_pairs = ATTN_WIDTH // PAIR_WIDTH
    blk = (SUPER_BLOCK, PAIR_WIDTH)
    prev = lambda i: jnp.maximum(i - 1, 0)
    return pl.pallas_call(
        _prompt_attn_body,
        grid=(T_PROMPT // SUPER_BLOCK, n_pairs),
        in_specs=[pl.BlockSpec(blk, lambda i, h: (i, h)),
                  pl.BlockSpec(blk, lambda i, h: (prev(i), n_pairs + h)),
                  pl.BlockSpec(blk, lambda i, h: (i, n_pairs + h)),
                  pl.BlockSpec(blk, lambda i, h: (prev(i), 2 * n_pairs + h)),
                  pl.BlockSpec(blk, lambda i, h: (i, 2 * n_pairs + h))],
        out_specs=pl.BlockSpec(blk, lambda i, h: (i, h)),
        out_shape=jax.ShapeDtypeStruct((T_PROMPT, ATTN_WIDTH), F32),
        scratch_shapes=[pltpu.VMEM((2 * SUPER_BLOCK, PAIR_WIDTH), F32)] * 2
                       + [pltpu.VMEM((len(DILATED_BRANCHES), SUPER_BLOCK, PAIR_WIDTH), F32)] * 2,
        compiler_params=_cparams("parallel", "parallel"),
        name="prompt_attention",
    )(z, z, z, z, z)


NEW_ROWS_PAD = 16
OUT_ROWS_PAD = 8


def _sample_attn_body(zs_ref, kt_ref, vt_ref, o_ref, k_scr, v_scr):
    s_len = DEC_SEQ
    rows = s_len * N_HEADS
    head_shift = N_HEADS.bit_length() - 1
    zs = zs_ref[...]
    q = zs[:, 0:ATTN_WIDTH]
    k_new = zs[:, ATTN_WIDTH:2 * ATTN_WIDTH].astype(BF16)
    v_new = zs[:, 2 * ATTN_WIDTH:3 * ATTN_WIDTH].astype(BF16)
    k_scr[...] = kt_ref[...].astype(BF16)
    v_scr[...] = vt_ref[...].astype(BF16)

    lane_head = jnp.right_shift(lax.broadcasted_iota(I32, (N_HEADS, ATTN_WIDTH), 1), HEAD_DIM.bit_length() - 1)
    head_mask = lane_head == lax.broadcasted_iota(I32, (N_HEADS, ATTN_WIDTH), 0)
    qbd = jnp.concatenate(
        [jnp.where(head_mask, jnp.broadcast_to(q[s:s + 1, :], (N_HEADS, ATTN_WIDTH)), 0.0)
         for s in range(s_len)], axis=0).astype(BF16)
    row_head = jnp.bitwise_and(lax.broadcasted_iota(I32, (rows, 1), 0), N_HEADS - 1)
    slope = jnp.exp2((row_head + 1).astype(F32) * SLOPE_LOG2_STEP)
    nt = (((1,), (1,)), ((), ()))

    sc = jnp.dot(qbd, k_scr[...], preferred_element_type=F32)
    dist = (PAST_LEN + jnp.right_shift(lax.broadcasted_iota(I32, (rows, PAST_LEN), 0), head_shift)
            - lax.broadcasted_iota(I32, (rows, PAST_LEN), 1))
    mult = jnp.zeros((rows, PAST_LEN), F32)
    for window, dil in DILATED_BRANCHES:
        mult = mult + jnp.where((dist <= window) & (jnp.bitwise_and(dist, dil - 1) == 0), 1.0, 0.0)
    sc = jnp.where(mult > 0, sc * ATTN_SCALE - slope * dist.astype(F32), -jnp.inf)

    sn = lax.dot_general(qbd, k_new, nt, preferred_element_type=F32)
    col_n = lax.broadcasted_iota(I32, (rows, NEW_ROWS_PAD), 1)
    dist_n = jnp.right_shift(lax.broadcasted_iota(I32, (rows, NEW_ROWS_PAD), 0), head_shift) - col_n
    mult_n = jnp.where((dist_n >= 0) & (col_n < s_len),
                       jnp.where(dist_n == 0, float(len(DILATED_BRANCHES)), 1.0), 0.0)
    sn = jnp.where(mult_n > 0, sn * ATTN_SCALE - slope * dist_n.astype(F32), -jnp.inf)

    m = jnp.maximum(jnp.max(sc, axis=-1, keepdims=True), jnp.max(sn, axis=-1, keepdims=True))
    pc = mult * jnp.exp(sc - m)
    pn = mult_n * jnp.exp(sn - m)
    l = jnp.sum(pc, axis=-1, keepdims=True) + jnp.sum(pn, axis=-1, keepdims=True)
    acc = (lax.dot_general(pc.astype(BF16), v_scr[...], nt, preferred_element_type=F32)
           + jnp.dot(pn.astype(BF16), v_new, preferred_element_type=F32)) / l
    o_ref[...] = jnp.zeros_like(o_ref)
    for s in range(s_len):
        o_ref[s:s + 1, :] = jnp.sum(jnp.where(head_mask, acc[s * N_HEADS:(s + 1) * N_HEADS, :], 0.0),
                                    axis=0, keepdims=True)


def _sample_attention(z, cache_kt, cache_vt, layer):
    zs = jnp.pad(z[T_PROMPT:].reshape(DEC_BATCH, DEC_SEQ, IN_COLS), ((0, 0), (0, NEW_ROWS_PAD - DEC_SEQ), (0, 0)))
    cache_spec = pl.BlockSpec((None, None, ATTN_WIDTH, PAST_LEN), lambda b: (layer, b, 0, 0))
    out = pl.pallas_call(
        _sample_attn_body,
        grid=(DEC_BATCH,),
        in_specs=[pl.BlockSpec((None, NEW_ROWS_PAD, IN_COLS), lambda b: (b, 0, 0)), cache_spec, cache_spec],
        out_specs=pl.BlockSpec((None, OUT_ROWS_PAD, ATTN_WIDTH), lambda b: (b, 0, 0)),
        out_shape=jax.ShapeDtypeStruct((DEC_BATCH, OUT_ROWS_PAD, ATTN_WIDTH), F32),
        scratch_shapes=[pltpu.VMEM((ATTN_WIDTH, PAST_LEN), BF16)] * 2,
        compiler_params=_cparams("parallel"),
        name="sample_attention",
    )(zs, cache_kt, cache_vt)
    return out[:, :DEC_SEQ].reshape(T_SAMPLE, ATTN_WIDTH)


def _ssm_matrices(a_re, a_im, log_dt, b_re, b_im, c_re, c_im, chunk):
    hp = lax.Precision.HIGHEST
    dt = jnp.exp(log_dt.astype(F32))[:, None]
    a_re, a_im = a_re.astype(F32), a_im.astype(F32)
    ks = jnp.arange(chunk + 1, dtype=F32)[:, None, None]
    mag = jnp.exp(ks * dt * a_re)
    pw_re, pw_im = mag * jnp.cos(ks * dt * a_im), mag * jnp.sin(ks * dt * a_im)
    den = a_re * a_re + a_im * a_im
    n_re, n_im = pw_re[1] - 1.0, pw_im[1]
    f_re, f_im = (n_re * a_re + n_im * a_im) / den, (n_im * a_re - n_re * a_im) / den
    b_re, b_im = b_re.astype(F32), b_im.astype(F32)
    bb_re = f_re[..., None] * b_re - f_im[..., None] * b_im
    bb_im = f_re[..., None] * b_im + f_im[..., None] * b_re
    c_re, c_im = c_re.astype(F32), c_im.astype(F32)
    ca_re = c_re[None] * pw_re[:, :, None, :] - c_im[None] * pw_im[:, :, None, :]
    ca_im = c_re[None] * pw_im[:, :, None, :] + c_im[None] * pw_re[:, :, None, :]
    kern = (jnp.einsum('lgcn,gnd->lgcd', ca_re[:chunk], bb_re, precision=hp)
            - jnp.einsum('lgcn,gnd->lgcd', ca_im[:chunk], bb_im, precision=hp))
    eye = jnp.eye(2, dtype=F32)
    pair = lambda x, ax: x.reshape(x.shape[:ax] + (N_PAIRS, 2) + x.shape[ax + 1:])
    tc = chunk * 2 * SSM_GROUP

    s_idx = jnp.arange(chunk)
    lag = s_idx[None, :] - s_idx[:, None]
    wt = jnp.where((lag >= 0)[:, :, None, None, None], kern[jnp.clip(lag, 0)], 0.0)
    w = jnp.einsum('stjgcd,gh->jshdtgc', pair(wt, 2), eye).reshape(N_PAIRS, tc, tc)

    rev_re, rev_im = pw_re[chunk - 1::-1][:chunk], pw_im[chunk - 1::-1][:chunk]
    p_re = rev_re[..., None] * bb_re[None] - rev_im[..., None] * bb_im[None]
    p_im = rev_re[..., None] * bb_im[None] + rev_im[..., None] * bb_re[None]
    p = jnp.einsum('rsjgnd,gh->jshdrgn', pair(jnp.stack([p_re, p_im]), 2), eye)
    p = p.reshape(N_PAIRS, tc, PAIR_STATE)

    r = jnp.einsum('rtjgcn,gh->jrhntgc', pair(jnp.stack([ca_re[1:], -ca_im[1:]]), 2), eye)
    r = r.reshape(N_PAIRS, PAIR_STATE, tc)

    at_re = pair(pw_re[chunk], 0).reshape(N_PAIRS, 2 * SSM_STATE)
    at_im = pair(pw_im[chunk], 0).reshape(N_PAIRS, 2 * SSM_STATE)
    m1 = jnp.concatenate([at_re, at_re], axis=-1)
    m2 = jnp.concatenate([-at_im, at_im], axis=-1)
    return w.astype(BF16), p.astype(BF16), r.astype(BF16), m1, m2


def _swap_halves(h, width):
    parts = []
    for c in range(h.shape[-1] // (2 * width)):
        parts += [h[..., (2 * c + 1) * width:(2 * c + 2) * width], h[..., 2 * c * width:(2 * c + 1) * width]]
    return jnp.concatenate(parts, axis=-1)


def _ssm_z_body(u_ref, p_ref, z_ref):
    z_ref[...] = jnp.dot(u_ref[...].astype(BF16), p_ref[...], preferred_element_type=F32)


def _ssm_scan_body(z_ref, m1_ref, m2_ref, hin_ref, hlast_ref, h_scr):
    @pl.when(pl.program_id(0) == 0)
    def _():
        h_scr[...] = jnp.zeros_like(h_scr)

    m1, m2 = m1_ref[...], m2_ref[...]

    def step(k, h):
        hin_ref[k] = h
        return m1 * h + m2 * _swap_halves(h, 2 * SSM_STATE) + z_ref[k]

    h = lax.fori_loop(0, z_ref.shape[0], step, h_scr[...])
    h_scr[...] = h
    hlast_ref[...] = h


def _ssm_y_body(u_ref, w_ref, hin_ref, r_ref, y_ref):
    y_ref[...] = (jnp.dot(u_ref[...].astype(BF16), w_ref[...], preferred_element_type=F32)
                  + jnp.dot(hin_ref[...].astype(BF16), r_ref[...], preferred_element_type=F32))


def _ssm_prompt(u, mats):
    w, p, r, m1, m2 = mats
    tc = SSM_CHUNK * 2 * SSM_GROUP
    u_pairs = u.reshape(N_CHUNKS, SSM_CHUNK, N_PAIRS, 2 * SSM_GROUP).transpose(2, 0, 1, 3)
    u_pairs = u_pairs.reshape(N_PAIRS, N_CHUNKS, tc)
    sub = 8
    state_w = N_PAIRS * PAIR_STATE // sub
    z = pl.pallas_call(
        _ssm_z_body,
        grid=(N_PAIRS,),
        in_specs=[pl.BlockSpec((None, N_CHUNKS, tc), lambda j: (j, 0, 0)),
                  pl.BlockSpec((None, tc, PAIR_STATE), lambda j: (j, 0, 0))],
        out_specs=pl.BlockSpec((N_CHUNKS, PAIR_STATE), lambda j: (0, j)),
        out_shape=jax.ShapeDtypeStruct((N_CHUNKS, N_PAIRS * PAIR_STATE), F32),
        compiler_params=_cparams("parallel"),
        name="ssm_chunk_state",
    )(u_pairs, p)
    scan_rows = 128
    hin, hlast = pl.pallas_call(
        _ssm_scan_body,
        grid=(N_CHUNKS // scan_rows,),
        in_specs=[pl.BlockSpec((scan_rows, sub, state_w), lambda i: (i, 0, 0)),
                  pl.BlockSpec((sub, state_w), lambda i: (0, 0)),
                  pl.BlockSpec((sub, state_w), lambda i: (0, 0))],
        out_specs=[pl.BlockSpec((scan_rows, sub, state_w), lambda i: (i, 0, 0)),
                   pl.BlockSpec((sub, state_w), lambda i: (0, 0))],
        out_shape=[jax.ShapeDtypeStruct((N_CHUNKS, sub, state_w), F32),
                   jax.ShapeDtypeStruct((sub, state_w), F32)],
        scratch_shapes=[pltpu.VMEM((sub, state_w), F32)],
        compiler_params=_cparams("arbitrary"),
        name="ssm_scan",
    )(z.reshape(N_CHUNKS, sub, state_w), m1.reshape(sub, state_w), m2.reshape(sub, state_w))
    y = pl.pallas_call(
        _ssm_y_body,
        grid=(N_PAIRS,),
        in_specs=[pl.BlockSpec((None, N_CHUNKS, tc), lambda j: (j, 0, 0)),
                  pl.BlockSpec((None, tc, tc), lambda j: (j, 0, 0)),
                  pl.BlockSpec((N_CHUNKS, PAIR_STATE), lambda j: (0, j)),
                  pl.BlockSpec((None, PAIR_STATE, tc), lambda j: (j, 0, 0))],
        out_specs=pl.BlockSpec((None, N_CHUNKS, tc), lambda j: (j, 0, 0)),
        out_shape=jax.ShapeDtypeStruct((N_PAIRS, N_CHUNKS, tc), F32),
        compiler_params=_cparams("parallel"),
        name="ssm_chunk_output",
    )(u_pairs, w, hin.reshape(N_CHUNKS, N_PAIRS * PAIR_STATE), r)
    y = y.reshape(N_PAIRS, N_CHUNKS, SSM_CHUNK, 2 * SSM_GROUP).transpose(1, 2, 0, 3)
    return y.reshape(T_PROMPT, SSM_WIDTH), hlast.reshape(N_PAIRS, PAIR_STATE)


def _ssm_sample_body(u_ref, h0_ref, p_ref, w_ref, r_ref, m1_ref, m2_ref, y_ref, hl_ref):
    u = u_ref[...].astype(BF16)
    h0 = h0_ref[...]
    y_ref[...] = (jnp.dot(u, w_ref[...], preferred_element_type=F32)
                  + jnp.dot(h0.astype(BF16), r_ref[...], preferred_element_type=F32))
    hl_ref[...] = (m1_ref[...] * h0 + m2_ref[...] * _swap_halves(h0, 2 * SSM_STATE)
                   + jnp.dot(u, p_ref[...], preferred_element_type=F32))


def _ssm_sample(u, h0, mats):
    w, p, r, m1, m2 = mats
    tc = DEC_SEQ * 2 * SSM_GROUP
    u_pairs = u.reshape(DEC_BATCH, DEC_SEQ, N_PAIRS, 2 * SSM_GROUP).transpose(2, 0, 1, 3)
    u_pairs = u_pairs.reshape(N_PAIRS, DEC_BATCH, tc)
    per_pair = lambda *shape: pl.BlockSpec((None,) + shape, lambda j: (j,) + (0,) * len(shape))
    y, hl = pl.pallas_call(
        _ssm_sample_body,
        grid=(N_PAIRS,),
        in_specs=[per_pair(DEC_BATCH, tc), per_pair(DEC_BATCH, PAIR_STATE), per_pair(tc, PAIR_STATE),
                  per_pair(tc, tc), per_pair(PAIR_STATE, tc), per_pair(1, PAIR_STATE), per_pair(1, PAIR_STATE)],
        out_specs=[per_pair(DEC_BATCH, tc), per_pair(DEC_BATCH, PAIR_STATE)],
        out_shape=[jax.ShapeDtypeStruct((N_PAIRS, DEC_BATCH, tc), F32),
                   jax.ShapeDtypeStruct((N_PAIRS, DEC_BATCH, PAIR_STATE), F32)],
        compiler_params=_cparams("parallel"),
        name="ssm_sample",
    )(u_pairs, h0, p, w, r, m1[:, None, :], m2[:, None, :])
    y = y.reshape(N_PAIRS, DEC_BATCH, DEC_SEQ, 2 * SSM_GROUP).transpose(1, 2, 0, 3)
    return y.reshape(T_SAMPLE, SSM_WIDTH), hl


def _state_to_pairs(re, im):
    b = re.shape[0]
    f = lambda x: x.astype(F32).reshape(b, N_PAIRS, 2 * SSM_STATE).transpose(1, 0, 2)
    return jnp.concatenate([f(re), f(im)], axis=-1)


def _pairs_to_state(h):
    b = h.shape[1]
    f = lambda x: x.transpose(1, 0, 2).reshape(b, N_SSM_GROUPS, SSM_STATE)
    return f(h[..., :2 * SSM_STATE]), f(h[..., 2 * SSM_STATE:])


def _glu_body(y_ref, u_ref, d_ref, w_ref, o_ref):
    y = jax.nn.gelu(y_ref[...] + d_ref[...] * u_ref[...])
    zz = jnp.dot(y.astype(BF16), w_ref[...], preferred_element_type=F32)
    val, gate = zz[:, :SSM_WIDTH], zz[:, SSM_WIDTH:]
    o_ref[...] = val / (1.0 + jnp.exp(-gate))


def _glu(y, z, d_skip, w_glu, layer):
    tm = 2 * ROW_TILE
    return pl.pallas_call(
        _glu_body,
        grid=(T_ALL // tm,),
        in_specs=[pl.BlockSpec((tm, SSM_WIDTH), lambda i: (i, 0)),
                  pl.BlockSpec((tm, SSM_WIDTH), lambda i: (i, 3 * ATTN_WIDTH // SSM_WIDTH)),
                  pl.BlockSpec((None, 1, SSM_WIDTH), lambda i: (layer, 0, 0)),
                  pl.BlockSpec((None, SSM_WIDTH, 2 * SSM_WIDTH), lambda i: (layer, 0, 0))],
        out_specs=pl.BlockSpec((tm, SSM_WIDTH), lambda i: (i, 0)),
        out_shape=jax.ShapeDtypeStruct((T_ALL, SSM_WIDTH), F32),
        compiler_params=_cparams("parallel"),
        name="ssm_glu",
    )(y, z, d_skip, w_glu)


def _layer_norm(x, g, b):
    mu = jnp.mean(x, axis=-1, keepdims=True)
    xc = x - mu
    var = jnp.mean(xc * xc, axis=-1, keepdims=True)
    return xc * lax.rsqrt(var + NORM_EPS) * g + b


def _rms_norm(x, g):
    return x * lax.rsqrt(jnp.mean(x * x, axis=-1, keepdims=True) + NORM_EPS) * g


def _merge_body(ap_ref, as_ref, ssm_ref, x_ref, ga_ref, gs_ref, wa_ref, ws_ref, g_ref, b_ref, o_ref):
    is_prompt = pl.program_id(0) < T_PROMPT // ROW_TILE
    attn = jnp.where(is_prompt, ap_ref[...], as_ref[...])
    an = _rms_norm(attn, ga_ref[...]).astype(BF16)
    sn = _rms_norm(ssm_ref[...], gs_ref[...]).astype(BF16)
    mix = (jnp.dot(an, wa_ref[...], preferred_element_type=F32)
           + jnp.dot(sn, ws_ref[...], preferred_element_type=F32))
    o_ref[...] = _layer_norm(DEEPNORM_ALPHA * x_ref[...] + mix, g_ref[...], b_ref[...])


def _merge(attn_p, attn_s, ssm, x, g_attn, g_ssm, w_out, ln_g, ln_b, layer):
    n_p = T_PROMPT // ROW_TILE
    row = lambda width: pl.BlockSpec((ROW_TILE, width), lambda i: (i, 0))
    vec = lambda width: pl.BlockSpec((None, 1, width), lambda i: (layer, 0, 0))
    half = lambda which: pl.BlockSpec((None, ATTN_WIDTH, D_MODEL), lambda i: (layer, which, 0))
    return pl.pallas_call(
        _merge_body,
        grid=(T_ALL // ROW_TILE,),
        in_specs=[pl.BlockSpec((ROW_TILE, ATTN_WIDTH), lambda i: (jnp.minimum(i, n_p - 1), 0)),
                  pl.BlockSpec((ROW_TILE, ATTN_WIDTH), lambda i: (jnp.maximum(i - n_p, 0), 0)),
                  row(SSM_WIDTH), row(D_MODEL), vec(ATTN_WIDTH), vec(SSM_WIDTH), half(0), half(1),
                  vec(D_MODEL), vec(D_MODEL)],
        out_specs=row(D_MODEL),
        out_shape=jax.ShapeDtypeStruct((T_ALL, D_MODEL), F32),
        compiler_params=_cparams("parallel"),
        name="merge_project_ln1",
    )(attn_p, attn_s, ssm, x, g_attn, g_ssm, w_out, w_out, ln_g, ln_b)


ROUTER_TILE = 512
INFO_ROWS = 8


def _top2_of4(a, b, c, d):
    m1, n1 = jnp.maximum(a, b), jnp.minimum(a, b)
    m2, n2 = jnp.maximum(c, d), jnp.minimum(c, d)
    return jnp.maximum(m1, m2) + jnp.maximum(jnp.minimum(m1, m2), jnp.maximum(n1, n2))


def _router_body(x_ref, wr_ref, br_ref, tri_ref, info_ref, cnt_ref, carry):
    @pl.when(pl.program_id(0) == 0)
    def _():
        carry[...] = jnp.zeros_like(carry)

    tm = x_ref.shape[0]
    logits = lax.dot_general(wr_ref[...], x_ref[...], (((1,), (1,)), ((), ())),
                             precision=lax.Precision.HIGHEST, preferred_element_type=F32)
    ex = jnp.exp(logits - jnp.max(logits, axis=0, keepdims=True))
    scores = ex / jnp.sum(ex, axis=0, keepdims=True)
    sel = scores + br_ref[...]
    row = lambda a, i: a[i:i + 1, :]
    epg = EXPERTS_PER_GROUP
    grp = [_top2_of4(*[row(sel, epg * g + i) for i in range(epg)]) for g in range(N_EXPERT_GROUPS)]
    best, best_v = jnp.zeros((1, tm), I32), grp[0]
    for g in range(1, N_EXPERT_GROUPS):
        upd = grp[g] > best_v
        best, best_v = jnp.where(upd, g, best), jnp.where(upd, grp[g], best_v)

    def in_group(a, i):
        out = row(a, i)
        for g in range(1, N_EXPERT_GROUPS):
            out = jnp.where(best == g, row(a, epg * g + i), out)
        return out

    v = [in_group(sel, i) for i in range(epg)]
    sc = [in_group(scores, i) for i in range(epg)]
    i0, v0, s0 = jnp.zeros((1, tm), I32), v[0], sc[0]
    for i in range(1, epg):
        upd = v[i] > v0
        i0, v0, s0 = jnp.where(upd, i, i0), jnp.where(upd, v[i], v0), jnp.where(upd, sc[i], s0)
    i1, v1, s1 = jnp.zeros((1, tm), I32), jnp.full((1, tm), -jnp.inf, F32), jnp.zeros((1, tm), F32)
    for i in range(epg):
        upd = (i0 != i) & (v[i] > v1)
        i1, v1, s1 = jnp.where(upd, i, i1), jnp.where(upd, v[i], v1), jnp.where(upd, sc[i], s1)
    e0, e1 = best * epg + i0, best * epg + i1
    wsum = s0 + s1

    eidx = lax.broadcasted_iota(I32, (N_EXPERTS, tm), 0)
    hit0, hit1 = eidx == e0, eidx == e1
    onehot = jnp.where(hit0 | hit1, 1.0, 0.0)
    before = jnp.dot(onehot.astype(BF16), tri_ref[...], preferred_element_type=F32) + carry[...]
    rank0 = jnp.sum(jnp.where(hit0, before, 0.0), axis=0, keepdims=True)
    rank1 = jnp.sum(jnp.where(hit1, before, 0.0), axis=0, keepdims=True)
    total = carry[...] + jnp.sum(onehot, axis=1, keepdims=True)
    carry[...] = total
    cnt_ref[...] = jnp.broadcast_to(total, cnt_ref.shape)
    info_ref[...] = jnp.zeros_like(info_ref)
    for i, val in enumerate((e0.astype(F32), e1.astype(F32), s0 / wsum, s1 / wsum, rank0, rank1)):
        info_ref[i:i + 1, :] = val


def _router(x, w_router_t, b_router, tri):
    tm = ROUTER_TILE
    return pl.pallas_call(
        _router_body,
        grid=(T_ALL // tm,),
        in_specs=[pl.BlockSpec((tm, D_MODEL), lambda i: (i, 0)),
                  pl.BlockSpec((N_EXPERTS, D_MODEL), lambda i: (0, 0)),
                  pl.BlockSpec((N_EXPERTS, 1), lambda i: (0, 0)),
                  pl.BlockSpec((tm, tm), lambda i: (0, 0))],
        out_specs=[pl.BlockSpec((INFO_ROWS, tm), lambda i: (0, i)),
                   pl.BlockSpec((N_EXPERTS, 128), lambda i: (0, 0))],
        out_shape=[jax.ShapeDtypeStruct((INFO_ROWS, T_ALL), F32),
                   jax.ShapeDtypeStruct((N_EXPERTS, 128), F32)],
        scratch_shapes=[pltpu.VMEM((N_EXPERTS, 1), F32)],
        compiler_params=_cparams("arbitrary"),
        name="router",
    )(x, w_router_t, b_router, tri)


def _routing_tables(info, counts):
    cnt = counts[:, 0].astype(I32)
    tiles = (cnt + EXPERT_TILE - 1) // EXPERT_TILE
    tiles_end = jnp.cumsum(tiles)
    first_slot = (tiles_end - tiles) * EXPERT_TILE
    e0, e1 = info[0].astype(I32), info[1].astype(I32)
    dest0 = first_slot[e0] + info[4].astype(I32)
    dest1 = first_slot[e1] + info[5].astype(I32)
    n_valid = tiles_end[-1]
    tile = jnp.minimum(jnp.arange(N_EXPERT_TILES, dtype=I32), n_valid - 1)
    tile_expert = jnp.sum((tiles_end[None, :] <= tile[:, None]).astype(I32), axis=1)
    tile_expert = jnp.minimum(tile_expert, N_EXPERTS - 1)
    last_tile = jnp.where(tiles > 0, tiles_end - 1, -1).astype(I32)
    return dest0, dest1, tile, tile_expert, n_valid.reshape(1).astype(I32), last_tile


def _row_copy(src_ref, src_row, dst_ref, dst_row, sem):
    return pltpu.make_async_copy(src_ref.at[pl.ds(src_row, 1)], dst_ref.at[pl.ds(dst_row, 1)], sem)


def _dispatch_body(dest0, dest1, last_tile, n_valid, x_ref, xs_ref, zero_scr, sem):
    step = pl.program_id(0)
    tm = x_ref.shape[0]

    @pl.when(step == 0)
    def _():
        zero_scr[...] = jnp.zeros_like(zero_scr)
        fill = lambda t: pltpu.make_async_copy(zero_scr, xs_ref.at[pl.ds(t * EXPERT_TILE, EXPERT_TILE)], sem)
        padded = [(last_tile[e], last_tile[e] >= 0) for e in range(N_EXPERTS)]
        unused = [(N_EXPERT_TILES - 1 - j, N_EXPERT_TILES - 1 - j >= n_valid[0]) for j in range(N_EXPERTS)]
        for t, needed in padded + unused:
            @pl.when(needed)
            def _():
                fill(t).start()
        for t, needed in padded + unused:
            @pl.when(needed)
            def _():
                fill(t).wait()

    base = step * tm

    def start(r, c):
        _row_copy(x_ref, r, xs_ref, dest0[base + r], sem).start()
        _row_copy(x_ref, r, xs_ref, dest1[base + r], sem).start()
        return c

    def wait(r, c):
        _row_copy(x_ref, r, xs_ref, dest0[base + r], sem).wait()
        _row_copy(x_ref, r, xs_ref, dest1[base + r], sem).wait()
        return c

    lax.fori_loop(0, tm, start, 0)
    lax.fori_loop(0, tm, wait, 0)


def _dispatch(x, dest0, dest1, last_tile, n_valid):
    return pl.pallas_call(
        _dispatch_body,
        grid_spec=pltpu.PrefetchScalarGridSpec(
            num_scalar_prefetch=4,
            grid=(T_ALL // ROW_TILE,),
            in_specs=[pl.BlockSpec((ROW_TILE, D_MODEL), lambda i, *_: (i, 0))],
            out_specs=pl.BlockSpec(memory_space=pl.ANY),
            scratch_shapes=[pltpu.VMEM((EXPERT_TILE, D_MODEL), F32), pltpu.SemaphoreType.DMA],
        ),
        out_shape=jax.ShapeDtypeStruct((N_SLOTS, D_MODEL), F32),
        compiler_params=_cparams("arbitrary"),
        name="moe_dispatch",
    )(dest0, dest1, last_tile, n_valid, x)


def _expert_body(tile, tile_expert, n_valid, x_ref, wg_ref, wu_ref, wd_ref, y_ref):
    used = pl.program_id(0) < n_valid[0]

    @pl.when(used)
    def _():
        x = x_ref[...].astype(BF16)
        g = jnp.dot(x, wg_ref[...], preferred_element_type=F32)
        u = jnp.dot(x, wu_ref[...], preferred_element_type=F32)
        hid = (g / (1.0 + jnp.exp(-g))) * u
        y_ref[...] = jnp.dot(hid.astype(BF16), wd_ref[...], preferred_element_type=F32)

    @pl.when(jnp.logical_not(used))
    def _():
        y_ref[...] = jnp.zeros_like(y_ref)


def _experts(xs, tile, tile_expert, n_valid, w_gate, w_up, w_down, layer):
    rows = pl.BlockSpec((EXPERT_TILE, D_MODEL), lambda i, tile, te, nv: (tile[i], 0))
    out_rows = pl.BlockSpec((EXPERT_TILE, D_MODEL), lambda i, tile, te, nv: (i, 0))
    weight = lambda k, n: pl.BlockSpec((None, None, k, n), lambda i, tile, te, nv: (layer, te[i], 0, 0))
    return pl.pallas_call(
        _expert_body,
        grid_spec=pltpu.PrefetchScalarGridSpec(
            num_scalar_prefetch=3,
            grid=(N_EXPERT_TILES,),
            in_specs=[rows, weight(D_MODEL, D_EXPERT), weight(D_MODEL, D_EXPERT), weight(D_EXPERT, D_MODEL)],
            out_specs=out_rows,
        ),
        out_shape=jax.ShapeDtypeStruct((N_SLOTS, D_MODEL), F32),
        compiler_params=_cparams("arbitrary"),
        name="moe_experts",
    )(tile, tile_expert, n_valid, xs, w_gate, w_up, w_down)


def _combine_ln_body(dest0, dest1, ys_ref, x_ref, w0_ref, w1_ref, g_ref, b_ref, o_ref, buf0, buf1, sem):
    tm = x_ref.shape[0]
    base = pl.program_id(0) * tm

    def start(r, c):
        _row_copy(ys_ref, dest0[base + r], buf0, r, sem).start()
        _row_copy(ys_ref, dest1[base + r], buf1, r, sem).start()
        return c

    def wait(r, c):
        _row_copy(ys_ref, dest0[base + r], buf0, r, sem).wait()
        _row_copy(ys_ref, dest1[base + r], buf1, r, sem).wait()
        return c

    lax.fori_loop(0, tm, start, 0)
    lax.fori_loop(0, tm, wait, 0)
    y = w0_ref[...] * buf0[...] + w1_ref[...] * buf1[...]
    o_ref[...] = _layer_norm(DEEPNORM_ALPHA * x_ref[...] + y, g_ref[...], b_ref[...])


def _combine_ln(ys, x, dest0, dest1, w0, w1, ln_g, ln_b, layer):
    row = lambda width: pl.BlockSpec((ROW_TILE, width), lambda i, *_: (i, 0))
    vec = pl.BlockSpec((None, 1, D_MODEL), lambda i, *_: (layer, 0, 0))
    return pl.pallas_call(
        _combine_ln_body,
        grid_spec=pltpu.PrefetchScalarGridSpec(
            num_scalar_prefetch=2,
            grid=(T_ALL // ROW_TILE,),
            in_specs=[pl.BlockSpec(memory_space=pl.ANY), row(D_MODEL), row(1), row(1), vec, vec],
            out_specs=row(D_MODEL),
            scratch_shapes=[pltpu.VMEM((ROW_TILE, D_MODEL), F32)] * 2 + [pltpu.SemaphoreType.DMA],
        ),
        out_shape=jax.ShapeDtypeStruct((T_ALL, D_MODEL), F32),
        compiler_params=_cparams("arbitrary"),
        name="moe_combine_ln2",
    )(dest0, dest1, ys, x, w0.reshape(T_ALL, 1), w1.reshape(T_ALL, 1), ln_g, ln_b)


def _moe_ln(x, router_params, w_gate, w_up, w_down, ln_g, ln_b, layer):
    info, counts = _router(x, *router_params)
    dest0, dest1, tile, tile_expert, n_valid, last_tile = _routing_tables(info, counts)
    xs = _dispatch(x, dest0, dest1, last_tile, n_valid)
    ys = _experts(xs, tile, tile_expert, n_valid, w_gate, w_up, w_down, layer)
    return _combine_ln(ys, x, dest0, dest1, info[2], info[3], ln_g, ln_b, layer)


def kernel(x_prompt, x_sample, cache_k, cache_v, state_ssm_re, state_ssm_im, w_in, w_out, g_attn_out, g_ssm_out, ssm_a_re, ssm_a_im, ssm_log_dt, ssm_b_re, ssm_b_im, ssm_c_re, ssm_c_im, ssm_d, w_glu, ln1_g, ln1_b, ln2_g, ln2_b, w_router, b_router, w_gate, w_up, w_down):
    x = jnp.concatenate([x_prompt.reshape(T_PROMPT, D_MODEL), x_sample.reshape(T_SAMPLE, D_MODEL)], axis=0)
    x = x.astype(F32)
    w_in, w_out, w_glu = w_in.astype(BF16), w_out.astype(BF16), w_glu.astype(BF16)
    w_gate, w_up, w_down = w_gate.astype(BF16), w_up.astype(BF16), w_down.astype(BF16)
    vecs = [v.astype(F32).reshape(DEPTH, 1, -1) for v in (g_attn_out, g_ssm_out, ssm_d, ln1_g, ln1_b, ln2_g, ln2_b)]
    g_attn_out, g_ssm_out, ssm_d, ln1_g, ln1_b, ln2_g, ln2_b = vecs
    router_params = (w_router.astype(F32).T, b_router.astype(F32).reshape(N_EXPERTS, 1),
                     (jnp.arange(ROUTER_TILE)[:, None] < jnp.arange(ROUTER_TILE)[None, :]).astype(BF16))
    to_rows_minor = lambda c: c.transpose(0, 1, 3, 4, 2).reshape(DEPTH, DEC_BATCH, ATTN_WIDTH, PAST_LEN)
    cache_kt, cache_vt = to_rows_minor(cache_k), to_rows_minor(cache_v)

    kp_out, vp_out, srp_out, sip_out = [], [], [], []
    ks_out, vs_out, srs_out, sis_out = [], [], [], []
    for l in range(DEPTH):
        z = _matmul(x, w_in, l, tm=2 * ROW_TILE, tn=ATTN_WIDTH, name="project_in")
        k_cols, v_cols = z[:, ATTN_WIDTH:2 * ATTN_WIDTH], z[:, 2 * ATTN_WIDTH:3 * ATTN_WIDTH]
        u = z[:, 3 * ATTN_WIDTH:]

        attn_p = _prompt_attention(z)
        attn_s = _sample_attention(z, cache_kt, cache_vt, l)

        ssm_params = (ssm_a_re[l], ssm_a_im[l], ssm_log_dt[l], ssm_b_re[l], ssm_b_im[l], ssm_c_re[l], ssm_c_im[l])
        y_p, h_p = _ssm_prompt(u[:T_PROMPT], _ssm_matrices(*ssm_params, SSM_CHUNK))
        y_s, h_s = _ssm_sample(u[T_PROMPT:], _state_to_pairs(state_ssm_re[l], state_ssm_im[l]),
                               _ssm_matrices(*ssm_params, DEC_SEQ))
        ssm = _glu(jnp.concatenate([y_p, y_s], axis=0), z, ssm_d, w_glu, l)

        x = _merge(attn_p, attn_s, ssm, x, g_attn_out, g_ssm_out, w_out, ln1_g, ln1_b, l)
        x = _moe_ln(x, router_params, w_gate, w_up, w_down, ln2_g, ln2_b, l)

        heads = lambda a, b: a.reshape(b, -1, N_HEADS, HEAD_DIM)
        window = min(2048, SEQ)
        kp_out.append(heads(k_cols[T_PROMPT - window:T_PROMPT], 1))
        vp_out.append(heads(v_cols[T_PROMPT - window:T_PROMPT], 1))
        ks_out.append(heads(k_cols[T_PROMPT:], DEC_BATCH))
        vs_out.append(heads(v_cols[T_PROMPT:], DEC_BATCH))
        re_p, im_p = _pairs_to_state(h_p[:, None, :])
        re_s, im_s = _pairs_to_state(h_s)
        srp_out.append(re_p)
        sip_out.append(im_p)
        srs_out.append(re_s)
        sis_out.append(im_s)
    return (x[:T_PROMPT].reshape(1, SEQ, D_MODEL), x[T_PROMPT:].reshape(DEC_BATCH, DEC_SEQ, D_MODEL),
            jnp.stack(kp_out), jnp.stack(vp_out), jnp.stack(srp_out), jnp.stack(sip_out),
            jnp.stack(ks_out), jnp.stack(vs_out), jnp.stack(srs_out), jnp.stack(sis_out))
```

```python
import functools
import math

import jax
import jax.numpy as jnp
from jax import lax
from jax.experimental import pallas as pl
from jax.experimental.pallas import tpu as pltpu

F32 = jnp.float32
BF16 = jnp.bfloat16
I32 = jnp.int32

D_MODEL = 2048
SEQ = 8192
DEPTH = 2
DEC_BATCH = 128
DEC_SEQ = 4
PAST_LEN = 2048
ATTN_WIDTH = D_MODEL // 2
SSM_WIDTH = D_MODEL - ATTN_WIDTH
HEAD_DIM = 64
N_HEADS = ATTN_WIDTH // HEAD_DIM
DILATED_BRANCHES = ((128, 1), (512, 4), (2048, 16))
LOOKBACK = 128
SSM_GROUP = 16
N_SSM_GROUPS = SSM_WIDTH // SSM_GROUP
N_PAIRS = N_SSM_GROUPS // 2
SSM_STATE = 64
PAIR_STATE = 4 * SSM_STATE
IN_COLS = 3 * ATTN_WIDTH + SSM_WIDTH
N_EXPERTS = 16
N_EXPERT_GROUPS = 4
EXPERTS_PER_GROUP = N_EXPERTS // N_EXPERT_GROUPS
D_EXPERT = D_MODEL // 2
DEEPNORM_ALPHA = (2.0 * DEPTH) ** 0.25
NORM_EPS = 1e-5

T_PROMPT = SEQ
T_SAMPLE = DEC_BATCH * DEC_SEQ
T_ALL = T_PROMPT + T_SAMPLE
SSM_CHUNK = 16
N_CHUNKS = T_PROMPT // SSM_CHUNK

V7X_VMEM_BYTES = 64 * 1024 * 1024
VMEM_LIMIT = V7X_VMEM_BYTES - 8 * 1024 * 1024

ROW_TILE = 256
EXPERT_TILE = 256
N_SLOTS = 2 * T_ALL + N_EXPERTS * EXPERT_TILE
N_EXPERT_TILES = N_SLOTS // EXPERT_TILE

ATTN_SCALE = 1.0 / math.sqrt(HEAD_DIM)
assert math.frexp(ATTN_SCALE)[0] == 0.5, "queries are pre-scaled before rounding, which needs a power of two"
SLOPE_LOG2_STEP = -8.0 / N_HEADS


def _cparams(*semantics):
    return pltpu.CompilerParams(dimension_semantics=semantics, vmem_limit_bytes=VMEM_LIMIT)


def _mm_body(x_ref, w_ref, o_ref):
    o_ref[...] = jnp.dot(x_ref[...].astype(BF16), w_ref[...], preferred_element_type=F32)


def _matmul(x, w, layer, *, tm, tn, name):
    m, k = x.shape
    n = w.shape[-1]
    return pl.pallas_call(
        _mm_body,
        grid=(n // tn, m // tm),
        in_specs=[pl.BlockSpec((tm, k), lambda j, i: (i, 0)),
                  pl.BlockSpec((None, k, tn), lambda j, i: (layer, 0, j))],
        out_specs=pl.BlockSpec((tm, tn), lambda j, i: (i, j)),
        out_shape=jax.ShapeDtypeStruct((m, n), F32),
        compiler_params=_cparams("parallel", "parallel"),
        name=name,
    )(x, w)


SUPER_BLOCK = 2048
PAIR_WIDTH = 2 * HEAD_DIM
BLOCK_UNROLL = 4


def _prompt_attn_body(q_ref, kp_ref, kc_ref, vp_ref, vc_ref, o_ref, k_all, v_all, o_br, lse_br):
    sb = pl.program_id(0)
    hp = pl.program_id(1)
    blk = LOOKBACK
    k_all[0:SUPER_BLOCK, :] = kp_ref[...]
    k_all[SUPER_BLOCK:2 * SUPER_BLOCK, :] = kc_ref[...]
    v_all[0:SUPER_BLOCK, :] = vp_ref[...]
    v_all[SUPER_BLOCK:2 * SUPER_BLOCK, :] = vc_ref[...]
    qi = lax.broadcasted_iota(I32, (blk, 2 * blk), 0)
    kj = lax.broadcasted_iota(I32, (blk, 2 * blk), 1)
    dist = qi + blk - kj
    in_band = (dist >= 0) & (dist <= LOOKBACK)
    low_lanes = lax.broadcasted_iota(I32, (blk, PAIR_WIDTH), 1) < HEAD_DIM
    head_lanes = (low_lanes, jnp.logical_not(low_lanes))
    slopes = [jnp.exp2(jnp.full((blk, 1), SLOPE_LOG2_STEP, F32) * (2 * hp + sub + 1).astype(F32))
              for sub in range(2)]
    nt = (((1,), (1,)), ((), ()))

    for bi, (_, dil) in enumerate(DILATED_BRANCHES):
        per_residue = SUPER_BLOCK // (blk * dil)
        bias = [slopes[sub] * (dist * dil).astype(F32) for sub in range(2)]
        stride = dil if dil > 1 else None

        def block(idx, carry, bi=bi, dil=dil, per_residue=per_residue, bias=bias, stride=stride):
            g = jnp.bitwise_and(idx, per_residue - 1)
            r = jnp.right_shift(idx, per_residue.bit_length() - 1)
            q0 = dil * blk * g + r
            kv0 = SUPER_BLOCK - dil * blk + q0
            if dil == 1:
                q0, kv0 = pl.multiple_of(q0, blk), pl.multiple_of(kv0, blk)
            q_rows = pl.ds(q0, blk, stride=stride)
            kv_rows = pl.ds(kv0, 2 * blk, stride=stride)
            first_key = jnp.where((sb == 0) & (g == 0), blk, 0)
            valid = in_band & (kj >= first_key)
            q2 = (q_ref[q_rows, :] * ATTN_SCALE).astype(BF16)
            k2 = k_all[kv_rows, :].astype(BF16)
            v2 = v_all[kv_rows, :].astype(BF16)
            outs, lses = [], []
            for sub in range(2):
                qm = jnp.where(head_lanes[sub], q2, jnp.zeros_like(q2))
                s = lax.dot_general(qm, k2, nt, preferred_element_type=F32)
                s = jnp.where(valid, s - bias[sub], -jnp.inf)
                m = jnp.max(s, axis=-1, keepdims=True)
                p = jnp.exp(s - m)
                l = jnp.sum(p, axis=-1, keepdims=True)
                o = jnp.dot(p.astype(BF16), v2, preferred_element_type=F32)
                outs.append(o / l)
                lses.append(jnp.broadcast_to(m + jnp.log(l), (blk, PAIR_WIDTH)))
            o_br[bi, q_rows, :] = jnp.where(low_lanes, outs[0], outs[1])
            lse_br[bi, q_rows, :] = jnp.where(low_lanes, lses[0], lses[1])
            return carry

        lax.fori_loop(0, dil * per_residue, block, 0, unroll=BLOCK_UNROLL)

    a, b, c = lse_br[0], lse_br[1], lse_br[2]
    mx = jnp.maximum(jnp.maximum(a, b), c)
    wa, wb, wc = jnp.exp(a - mx), jnp.exp(b - mx), jnp.exp(c - mx)
    o_ref[...] = (wa * o_br[0] + wb * o_br[1] + wc * o_br[2]) / (wa + wb + wc)


def _prompt_attention(z):
    n_pairs = ATTN_WIDTH // PAIR_WIDTH
    blk = (SUPER_BLOCK, PAIR_WIDTH)
    prev = lambda i: jnp.maximum(i - 1, 0)
    return pl.pallas_call(
        _prompt_attn_body,
        grid=(T_PROMPT // SUPER_BLOCK, n_pairs),
        in_specs=[pl.BlockSpec(blk, lambda i, h: (i, h)),
                  pl.BlockSpec(blk, lambda i, h: (prev(i), n_pairs + h)),
                  pl.BlockSpec(blk, lambda i, h: (i, n_pairs + h)),
                  pl.BlockSpec(blk, lambda i, h: (prev(i), 2 * n_pairs + h)),
                  pl.BlockSpec(blk, lambda i, h: (i, 2 * n_pairs + h))],
        out_specs=pl.BlockSpec(blk, lambda i, h: (i, h)),
        out_shape=jax.ShapeDtypeStruct((T_PROMPT, ATTN_WIDTH), F32),
        scratch_shapes=[pltpu.VMEM((2 * SUPER_BLOCK, PAIR_WIDTH), F32)] * 2
                       + [pltpu.VMEM((len(DILATED_BRANCHES), SUPER_BLOCK, PAIR_WIDTH), F32)] * 2,
        compiler_params=_cparams("parallel", "parallel"),
        name="prompt_attention",
    )(z, z, z, z, z)


NEW_ROWS_PAD = 16
OUT_ROWS_PAD = 8


def _sample_attn_body(zs_ref, kt_ref, vt_ref, o_ref, k_scr, v_scr):
    s_len = DEC_SEQ
    rows = s_len * N_HEADS
    head_shift = N_HEADS.bit_length() - 1
    zs = zs_ref[...]
    q = zs[:, 0:ATTN_WIDTH]
    k_new = zs[:, ATTN_WIDTH:2 * ATTN_WIDTH].astype(BF16)
    v_new = zs[:, 2 * ATTN_WIDTH:3 * ATTN_WIDTH].astype(BF16)
    k_scr[...] = kt_ref[...].astype(BF16)
    v_scr[...] = vt_ref[...].astype(BF16)

    lane_head = jnp.right_shift(lax.broadcasted_iota(I32, (N_HEADS, ATTN_WIDTH), 1), HEAD_DIM.bit_length() - 1)
    head_mask = lane_head == lax.broadcasted_iota(I32, (N_HEADS, ATTN_WIDTH), 0)
    qbd = jnp.concatenate(
        [jnp.where(head_mask, jnp.broadcast_to(q[s:s + 1, :], (N_HEADS, ATTN_WIDTH)), 0.0)
         for s in range(s_len)], axis=0).astype(BF16)
    row_head = jnp.bitwise_and(lax.broadcasted_iota(I32, (rows, 1), 0), N_HEADS - 1)
    slope = jnp.exp2((row_head + 1).astype(F32) * SLOPE_LOG2_STEP)
    nt = (((1,), (1,)), ((), ()))

    sc = jnp.dot(qbd, k_scr[...], preferred_element_type=F32)
    dist = (PAST_LEN + jnp.right_shift(lax.broadcasted_iota(I32, (rows, PAST_LEN), 0), head_shift)
            - lax.broadcasted_iota(I32, (rows, PAST_LEN), 1))
    mult = jnp.zeros((rows, PAST_LEN), F32)
    for window, dil in DILATED_BRANCHES:
        mult = mult + jnp.where((dist <= window) & (jnp.bitwise_and(dist, dil - 1) == 0), 1.0, 0.0)
    sc = jnp.where(mult > 0, sc * ATTN_SCALE - slope * dist.astype(F32), -jnp.inf)

    sn = lax.dot_general(qbd, k_new, nt, preferred_element_type=F32)
    col_n = lax.broadcasted_iota(I32, (rows, NEW_ROWS_PAD), 1)
    dist_n = jnp.right_shift(lax.broadcasted_iota(I32, (rows, NEW_ROWS_PAD), 0), head_shift) - col_n
    mult_n = jnp.where((dist_n >= 0) & (col_n < s_len),
                       jnp.where(dist_n == 0, float(len(DILATED_BRANCHES)), 1.0), 0.0)
    sn = jnp.where(mult_n > 0, sn * ATTN_SCALE - slope * dist_n.astype(F32), -jnp.inf)

    m = jnp.maximum(jnp.max(sc, axis=-1, keepdims=True), jnp.max(sn, axis=-1, keepdims=True))
    pc = mult * jnp.exp(sc - m)
    pn = mult_n * jnp.exp(sn - m)
    l = jnp.sum(pc, axis=-1, keepdims=True) + jnp.sum(pn, axis=-1, keepdims=True)
    acc = (lax.dot_general(pc.astype(BF16), v_scr[...], nt, preferred_element_type=F32)
           + jnp.dot(pn.astype(BF16), v_new, preferred_element_type=F32)) / l
    o_ref[...] = jnp.zeros_like(o_ref)
    for s in range(s_len):
        o_ref[s:s + 1, :] = jnp.sum(jnp.where(head_mask, acc[s * N_HEADS:(s + 1) * N_HEADS, :], 0.0),
                                    axis=0, keepdims=True)


def _sample_attention(z, cache_kt, cache_vt, layer):
    zs = jnp.pad(z[T_PROMPT:].reshape(DEC_BATCH, DEC_SEQ, IN_COLS), ((0, 0), (0, NEW_ROWS_PAD - DEC_SEQ), (0, 0)))
    cache_spec = pl.BlockSpec((None, None, ATTN_WIDTH, PAST_LEN), lambda b: (layer, b, 0, 0))
    out = pl.pallas_call(
        _sample_attn_body,
        grid=(DEC_BATCH,),
        in_specs=[pl.BlockSpec((None, NEW_ROWS_PAD, IN_COLS), lambda b: (b, 0, 0)), cache_spec, cache_spec],
        out_specs=pl.BlockSpec((None, OUT_ROWS_PAD, ATTN_WIDTH), lambda b: (b, 0, 0)),
        out_shape=jax.ShapeDtypeStruct((DEC_BATCH, OUT_ROWS_PAD, ATTN_WIDTH), F32),
        scratch_shapes=[pltpu.VMEM((ATTN_WIDTH, PAST_LEN), BF16)] * 2,
        compiler_params=_cparams("parallel"),
        name="sample_attention",
    )(zs, cache_kt, cache_vt)
    return out[:, :DEC_SEQ].reshape(T_SAMPLE, ATTN_WIDTH)


def _ssm_tables(a_re, a_im, log_dt, b_re, b_im, c_re, c_im):
    depth = a_re.shape[0]
    dt = jnp.exp(log_dt.astype(F32))[:, None, :, None]
    a_re, a_im = a_re.astype(F32)[:, None], a_im.astype(F32)[:, None]

    def powers(ks):
        e = ks.astype(F32)[None, :, None, None]
        mag = jnp.exp(e * dt * a_re)
        return mag * jnp.cos(e * dt * a_im), mag * jnp.sin(e * dt * a_im)

    pos_re, pos_im = powers(jnp.arange(SSM_CHUNK + 1))
    neg_re, neg_im = powers(-jnp.arange(SSM_CHUNK))
    den = (a_re * a_re + a_im * a_im)[:, 0]
    n_re, n_im = pos_re[:, 1] - 1.0, pos_im[:, 1]
    f_re = (n_re * a_re[:, 0] + n_im * a_im[:, 0]) / den
    f_im = (n_im * a_re[:, 0] - n_re * a_im[:, 0]) / den
    b_re, b_im = b_re.astype(F32), b_im.astype(F32)
    bb_re = (f_re[..., None] * b_re - f_im[..., None] * b_im).transpose(0, 1, 3, 2)
    bb_im = (f_re[..., None] * b_im + f_im[..., None] * b_re).transpose(0, 1, 3, 2)

    def lanes(re, im):
        x = jnp.stack([re, im], axis=1)
        return x.reshape(depth, 2, -1, N_PAIRS, 2 * SSM_STATE).transpose(0, 3, 1, 2, 4)

    def own_lanes(re, im):
        x = jnp.stack([re, im], axis=1).reshape(depth, 2, N_PAIRS, 2, SSM_GROUP, SSM_STATE)
        x = jnp.einsum('drjgcn,gh->djrgchn', x, jnp.eye(2, dtype=F32))
        return x.reshape(depth, N_PAIRS, 2, 2, SSM_GROUP, 2 * SSM_STATE)

    return (lanes(pos_re, pos_im), lanes(neg_re, neg_im), own_lanes(bb_re, bb_im),
            own_lanes(c_re.astype(F32), c_im.astype(F32)))


def _ssm_build_body(pos_ref, neg_ref, bb_ref, cc_ref, w_ref, p_ref, rt_ref, x_scr, y_scr, *, chunk):
    rows = chunk * SSM_GROUP
    cmul = lambda ar, ai, br, bi: (ar * br - ai * bi, ar * bi + ai * br)
    power = lambda ref, k: (ref[0, k:k + 1, :], ref[1, k:k + 1, :])
    for gg in range(2):
        b_r, b_i, c_r, c_i = bb_ref[0, gg], bb_ref[1, gg], cc_ref[0, gg], cc_ref[1, gg]
        for s in range(chunk):
            sl = slice(gg * rows + s * SSM_GROUP, gg * rows + (s + 1) * SSM_GROUP)
            p_r, p_i = cmul(*power(pos_ref, chunk - 1 - s), b_r, b_i)
            p_ref[sl, :] = jnp.concatenate([p_r, p_i], axis=1).astype(BF16)
            x_r, x_i = cmul(*power(neg_ref, s), b_r, b_i)
            x_scr[sl, :] = jnp.concatenate([x_r, x_i], axis=1)
            y_r, y_i = cmul(*power(pos_ref, s), c_r, c_i)
            y_scr[sl, :] = jnp.concatenate([y_r, -y_i], axis=1)
            r_r, r_i = cmul(*power(pos_ref, s + 1), c_r, c_i)
            rt_ref[sl, :] = jnp.concatenate([r_r, -r_i], axis=1).astype(BF16)
    w = lax.dot_general(x_scr[...], y_scr[...], (((1,), (1,)), ((), ())),
                        precision=lax.Precision.HIGHEST, preferred_element_type=F32)
    tok = lambda dim: jnp.bitwise_and(lax.broadcasted_iota(I32, (2 * rows, 2 * rows), dim), rows - 1)
    w_ref[...] = jnp.where(tok(1) >= jnp.bitwise_and(tok(0), -SSM_GROUP), w, 0.0).astype(BF16)


def _ssm_matrices(tables, layer, chunk):
    pos, neg, bb, cc = tables
    tc = 2 * chunk * SSM_GROUP
    table = lambda t: pl.BlockSpec((None, None) + t.shape[2:], lambda j: (layer, j) + (0,) * (t.ndim - 2))
    out = lambda width: pl.BlockSpec((None, tc, width), lambda j: (j, 0, 0))
    w, p, rt = pl.pallas_call(
        functools.partial(_ssm_build_body, chunk=chunk),
        grid=(N_PAIRS,),
        in_specs=[table(pos), table(neg), table(bb), table(cc)],
        out_specs=[out(tc), out(PAIR_STATE), out(PAIR_STATE)],
        out_shape=[jax.ShapeDtypeStruct((N_PAIRS, tc, tc), BF16),
                   jax.ShapeDtypeStruct((N_PAIRS, tc, PAIR_STATE), BF16),
                   jax.ShapeDtypeStruct((N_PAIRS, tc, PAIR_STATE), BF16)],
        scratch_shapes=[pltpu.VMEM((tc, PAIR_STATE), F32)] * 2,
        compiler_params=_cparams("parallel"),
        name="ssm_build",
    )(pos, neg, bb, cc)
    at_re, at_im = pos[layer, :, 0, chunk], pos[layer, :, 1, chunk]
    m1 = jnp.concatenate([at_re, at_re], axis=-1)
    m2 = jnp.concatenate([-at_im, at_im], axis=-1)
    return w, p, rt, m1, m2


def _swap_halves(h, width):
    parts = []
    for c in range(h.shape[-1] // (2 * width)):
        parts += [h[..., (2 * c + 1) * width:(2 * c + 2) * width], h[..., 2 * c * width:(2 * c + 1) * width]]
    return jnp.concatenate(parts, axis=-1)


def _ssm_z_body(u_ref, p_ref, z_ref):
    z_ref[...] = jnp.dot(u_ref[...].astype(BF16), p_ref[...], preferred_element_type=F32)


def _ssm_scan_body(z_ref, m1_ref, m2_ref, hin_ref, hlast_ref, h_scr):
    @pl.when(pl.program_id(0) == 0)
    def _():
        h_scr[...] = jnp.zeros_like(h_scr)

    m1, m2 = m1_ref[...], m2_ref[...]

    def step(k, h):
        hin_ref[k] = h
        return m1 * h + m2 * _swap_halves(h, 2 * SSM_STATE) + z_ref[k]

    h = lax.fori_loop(0, z_ref.shape[0], step, h_scr[...])
    h_scr[...] = h
    hlast_ref[...] = h


_NT = (((1,), (1,)), ((), ()))


def _ssm_y_body(u_ref, w_ref, hin_ref, rt_ref, y_ref):
    y_ref[...] = (jnp.dot(u_ref[...].astype(BF16), w_ref[...], preferred_element_type=F32)
                  + lax.dot_general(hin_ref[...].astype(BF16), rt_ref[...], _NT, preferred_element_type=F32))


def _to_pairs(u, n_chunks, chunk):
    u = u.astype(BF16).reshape(n_chunks, chunk, N_PAIRS, 2, SSM_GROUP).transpose(2, 0, 3, 1, 4)
    return u.reshape(N_PAIRS, n_chunks, 2 * chunk * SSM_GROUP)


def _from_pairs(y, n_chunks, chunk):
    y = y.reshape(N_PAIRS, n_chunks, 2, chunk, SSM_GROUP).transpose(1, 3, 0, 2, 4)
    return y.reshape(n_chunks * chunk, SSM_WIDTH)


def _ssm_prompt(u, mats):
    w, p, rt, m1, m2 = mats
    tc = SSM_CHUNK * 2 * SSM_GROUP
    u_pairs = _to_pairs(u, N_CHUNKS, SSM_CHUNK)
    sub = 8
    state_w = N_PAIRS * PAIR_STATE // sub
    z = pl.pallas_call(
        _ssm_z_body,
        grid=(N_PAIRS,),
        in_specs=[pl.BlockSpec((None, N_CHUNKS, tc), lambda j: (j, 0, 0)),
                  pl.BlockSpec((None, tc, PAIR_STATE), lambda j: (j, 0, 0))],
        out_specs=pl.BlockSpec((N_CHUNKS, PAIR_STATE), lambda j: (0, j)),
        out_shape=jax.ShapeDtypeStruct((N_CHUNKS, N_PAIRS * PAIR_STATE), F32),
        compiler_params=_cparams("parallel"),
        name="ssm_chunk_state",
    )(u_pairs, p)
    scan_rows = 128
    hin, hlast = pl.pallas_call(
        _ssm_scan_body,
        grid=(N_CHUNKS // scan_rows,),
        in_specs=[pl.BlockSpec((scan_rows, sub, state_w), lambda i: (i, 0, 0)),
                  pl.BlockSpec((sub, state_w), lambda i: (0, 0)),
                  pl.BlockSpec((sub, state_w), lambda i: (0, 0))],
        out_specs=[pl.BlockSpec((scan_rows, sub, state_w), lambda i: (i, 0, 0)),
                   pl.BlockSpec((sub, state_w), lambda i: (0, 0))],
        out_shape=[jax.ShapeDtypeStruct((N_CHUNKS, sub, state_w), F32),
                   jax.ShapeDtypeStruct((sub, state_w), F32)],
        scratch_shapes=[pltpu.VMEM((sub, state_w), F32)],
        compiler_params=_cparams("arbitrary"),
        name="ssm_scan",
    )(z.reshape(N_CHUNKS, sub, state_w), m1.reshape(sub, state_w), m2.reshape(sub, state_w))
    y = pl.pallas_call(
        _ssm_y_body,
        grid=(N_PAIRS,),
        in_specs=[pl.BlockSpec((None, N_CHUNKS, tc), lambda j: (j, 0, 0)),
                  pl.BlockSpec((None, tc, tc), lambda j: (j, 0, 0)),
                  pl.BlockSpec((N_CHUNKS, PAIR_STATE), lambda j: (0, j)),
                  pl.BlockSpec((None, tc, PAIR_STATE), lambda j: (j, 0, 0))],
        out_specs=pl.BlockSpec((None, N_CHUNKS, tc), lambda j: (j, 0, 0)),
        out_shape=jax.ShapeDtypeStruct((N_PAIRS, N_CHUNKS, tc), F32),
        compiler_params=_cparams("parallel"),
        name="ssm_chunk_output",
    )(u_pairs, w, hin.reshape(N_CHUNKS, N_PAIRS * PAIR_STATE), rt)
    return _from_pairs(y, N_CHUNKS, SSM_CHUNK), hlast.reshape(N_PAIRS, PAIR_STATE)


def _ssm_sample_body(u_ref, h0_ref, p_ref, w_ref, rt_ref, m1_ref, m2_ref, y_ref, hl_ref):
    u = u_ref[...].astype(BF16)
    h0 = h0_ref[...]
    y_ref[...] = (jnp.dot(u, w_ref[...], preferred_element_type=F32)
                  + lax.dot_general(h0.astype(BF16), rt_ref[...], _NT, preferred_element_type=F32))
    hl_ref[...] = (m1_ref[...] * h0 + m2_ref[...] * _swap_halves(h0, 2 * SSM_STATE)
                   + jnp.dot(u, p_ref[...], preferred_element_type=F32))


def _ssm_sample(u, h0, mats):
    w, p, rt, m1, m2 = mats
    tc = DEC_SEQ * 2 * SSM_GROUP
    u_pairs = _to_pairs(u, DEC_BATCH, DEC_SEQ)
    per_pair = lambda *shape: pl.BlockSpec((None,) + shape, lambda j: (j,) + (0,) * len(shape))
    y, hl = pl.pallas_call(
        _ssm_sample_body,
        grid=(N_PAIRS,),
        in_specs=[per_pair(DEC_BATCH, tc), per_pair(DEC_BATCH, PAIR_STATE), per_pair(tc, PAIR_STATE),
                  per_pair(tc, tc), per_pair(tc, PAIR_STATE), per_pair(1, PAIR_STATE), per_pair(1, PAIR_STATE)],
        out_specs=[per_pair(DEC_BATCH, tc), per_pair(DEC_BATCH, PAIR_STATE)],
        out_shape=[jax.ShapeDtypeStruct((N_PAIRS, DEC_BATCH, tc), F32),
                   jax.ShapeDtypeStruct((N_PAIRS, DEC_BATCH, PAIR_STATE), F32)],
        compiler_params=_cparams("parallel"),
        name="ssm_sample",
    )(u_pairs, h0, p, w, rt, m1[:, None, :], m2[:, None, :])
    return _from_pairs(y, DEC_BATCH, DEC_SEQ), hl


def _state_to_pairs(re, im):
    b = re.shape[0]
    f = lambda x: x.astype(F32).reshape(b, N_PAIRS, 2 * SSM_STATE).transpose(1, 0, 2)
    return jnp.concatenate([f(re), f(im)], axis=-1)


def _pairs_to_state(h):
    b = h.shape[1]
    f = lambda x: x.transpose(1, 0, 2).reshape(b, N_SSM_GROUPS, SSM_STATE)
    return f(h[..., :2 * SSM_STATE]), f(h[..., 2 * SSM_STATE:])


def _glu_body(y_ref, u_ref, d_ref, w_ref, o_ref):
    y = jax.nn.gelu(y_ref[...] + d_ref[...] * u_ref[...])
    zz = jnp.dot(y.astype(BF16), w_ref[...], preferred_element_type=F32)
    val, gate = zz[:, :SSM_WIDTH], zz[:, SSM_WIDTH:]
    o_ref[...] = val / (1.0 + jnp.exp(-gate))


def _glu(y, z, d_skip, w_glu, layer):
    tm = 2 * ROW_TILE
    return pl.pallas_call(
        _glu_body,
        grid=(T_ALL // tm,),
        in_specs=[pl.BlockSpec((tm, SSM_WIDTH), lambda i: (i, 0)),
                  pl.BlockSpec((tm, SSM_WIDTH), lambda i: (i, 3 * ATTN_WIDTH // SSM_WIDTH)),
                  pl.BlockSpec((None, 1, SSM_WIDTH), lambda i: (layer, 0, 0)),
                  pl.BlockSpec((None, SSM_WIDTH, 2 * SSM_WIDTH), lambda i: (layer, 0, 0))],
        out_specs=pl.BlockSpec((tm, SSM_WIDTH), lambda i: (i, 0)),
        out_shape=jax.ShapeDtypeStruct((T_ALL, SSM_WIDTH), F32),
        compiler_params=_cparams("parallel"),
        name="ssm_glu",
    )(y, z, d_skip, w_glu)


def _layer_norm(x, g, b):
    mu = jnp.mean(x, axis=-1, keepdims=True)
    xc = x - mu
    var = jnp.mean(xc * xc, axis=-1, keepdims=True)
    return xc * lax.rsqrt(var + NORM_EPS) * g + b


def _rms_norm(x, g):
    return x * lax.rsqrt(jnp.mean(x * x, axis=-1, keepdims=True) + NORM_EPS) * g


def _merge_body(ap_ref, as_ref, ssm_ref, x_ref, ga_ref, gs_ref, wa_ref, ws_ref, g_ref, b_ref, o_ref):
    is_prompt = pl.program_id(0) < T_PROMPT // ROW_TILE
    attn = jnp.where(is_prompt, ap_ref[...], as_ref[...])
    an = _rms_norm(attn, ga_ref[...]).astype(BF16)
    sn = _rms_norm(ssm_ref[...], gs_ref[...]).astype(BF16)
    mix = (jnp.dot(an, wa_ref[...], preferred_element_type=F32)
           + jnp.dot(sn, ws_ref[...], preferred_element_type=F32))
    o_ref[...] = _layer_norm(DEEPNORM_ALPHA * x_ref[...] + mix, g_ref[...], b_ref[...])


def _merge(attn_p, attn_s, ssm, x, g_attn, g_ssm, w_out, ln_g, ln_b, layer):
    n_p = T_PROMPT // ROW_TILE
    row = lambda width: pl.BlockSpec((ROW_TILE, width), lambda i: (i, 0))
    vec = lambda width: pl.BlockSpec((None, 1, width), lambda i: (layer, 0, 0))
    half = lambda which: pl.BlockSpec((None, ATTN_WIDTH, D_MODEL), lambda i: (layer, which, 0))
    return pl.pallas_call(
        _merge_body,
        grid=(T_ALL // ROW_TILE,),
        in_specs=[pl.BlockSpec((ROW_TILE, ATTN_WIDTH), lambda i: (jnp.minimum(i, n_p - 1), 0)),
                  pl.BlockSpec((ROW_TILE, ATTN_WIDTH), lambda i: (jnp.maximum(i - n_p, 0), 0)),
                  row(SSM_WIDTH), row(D_MODEL), vec(ATTN_WIDTH), vec(SSM_WIDTH), half(0), half(1),
                  vec(D_MODEL), vec(D_MODEL)],
        out_specs=row(D_MODEL),
        out_shape=jax.ShapeDtypeStruct((T_ALL, D_MODEL), F32),
        compiler_params=_cparams("parallel"),
        name="merge_project_ln1",
    )(attn_p, attn_s, ssm, x, g_attn, g_ssm, w_out, w_out, ln_g, ln_b)


ROUTER_TILE = 512
INFO_ROWS = 8


def _top2_of4(a, b, c, d):
    m1, n1 = jnp.maximum(a, b), jnp.minimum(a, b)
    m2, n2 = jnp.maximum(c, d), jnp.minimum(c, d)
    return jnp.maximum(m1, m2) + jnp.maximum(jnp.minimum(m1, m2), jnp.maximum(n1, n2))


def _router_body(x_ref, wr_ref, br_ref, tri_ref, info_ref, cnt_ref, carry):
    @pl.when(pl.program_id(0) == 0)
    def _():
        carry[...] = jnp.zeros_like(carry)

    tm = x_ref.shape[0]
    logits = lax.dot_general(wr_ref[...], x_ref[...], (((1,), (1,)), ((), ())),
                             precision=lax.Precision.HIGHEST, preferred_element_type=F32)
    ex = jnp.exp(logits - jnp.max(logits, axis=0, keepdims=True))
    scores = ex / jnp.sum(ex, axis=0, keepdims=True)
    sel = scores + br_ref[...]
    row = lambda a, i: a[i:i + 1, :]
    epg = EXPERTS_PER_GROUP
    grp = [_top2_of4(*[row(sel, epg * g + i) for i in range(epg)]) for g in range(N_EXPERT_GROUPS)]
    best, best_v = jnp.zeros((1, tm), I32), grp[0]
    for g in range(1, N_EXPERT_GROUPS):
        upd = grp[g] > best_v
        best, best_v = jnp.where(upd, g, best), jnp.where(upd, grp[g], best_v)

    def in_group(a, i):
        out = row(a, i)
        for g in range(1, N_EXPERT_GROUPS):
            out = jnp.where(best == g, row(a, epg * g + i), out)
        return out

    v = [in_group(sel, i) for i in range(epg)]
    sc = [in_group(scores, i) for i in range(epg)]
    i0, v0, s0 = jnp.zeros((1, tm), I32), v[0], sc[0]
    for i in range(1, epg):
        upd = v[i] > v0
        i0, v0, s0 = jnp.where(upd, i, i0), jnp.where(upd, v[i], v0), jnp.where(upd, sc[i], s0)
    i1, v1, s1 = jnp.zeros((1, tm), I32), jnp.full((1, tm), -jnp.inf, F32), jnp.zeros((1, tm), F32)
    for i in range(epg):
        upd = (i0 != i) & (v[i] > v1)
        i1, v1, s1 = jnp.where(upd, i, i1), jnp.where(upd, v[i], v1), jnp.where(upd, sc[i], s1)
    e0, e1 = best * epg + i0, best * epg + i1
    wsum = s0 + s1

    eidx = lax.broadcasted_iota(I32, (N_EXPERTS, tm), 0)
    hit0, hit1 = eidx == e0, eidx == e1
    onehot = jnp.where(hit0 | hit1, 1.0, 0.0)
    before = jnp.dot(onehot.astype(BF16), tri_ref[...], preferred_element_type=F32) + carry[...]
    rank0 = jnp.sum(jnp.where(hit0, before, 0.0), axis=0, keepdims=True)
    rank1 = jnp.sum(jnp.where(hit1, before, 0.0), axis=0, keepdims=True)
    total = carry[...] + jnp.sum(onehot, axis=1, keepdims=True)
    carry[...] = total
    cnt_ref[...] = jnp.broadcast_to(total, cnt_ref.shape)
    info_ref[...] = jnp.zeros_like(info_ref)
    for i, val in enumerate((e0.astype(F32), e1.astype(F32), s0 / wsum, s1 / wsum, rank0, rank1)):
        info_ref[i:i + 1, :] = val


def _router(x, w_router_t, b_router, tri):
    tm = ROUTER_TILE
    return pl.pallas_call(
        _router_body,
        grid=(T_ALL // tm,),
        in_specs=[pl.BlockSpec((tm, D_MODEL), lambda i: (i, 0)),
                  pl.BlockSpec((N_EXPERTS, D_MODEL), lambda i: (0, 0)),
                  pl.BlockSpec((N_EXPERTS, 1), lambda i: (0, 0)),
                  pl.BlockSpec((tm, tm), lambda i: (0, 0))],
        out_specs=[pl.BlockSpec((INFO_ROWS, tm), lambda i: (0, i)),
                   pl.BlockSpec((N_EXPERTS, 128), lambda i: (0, 0))],
        out_shape=[jax.ShapeDtypeStruct((INFO_ROWS, T_ALL), F32),
                   jax.ShapeDtypeStruct((N_EXPERTS, 128), F32)],
        scratch_shapes=[pltpu.VMEM((N_EXPERTS, 1), F32)],
        compiler_params=_cparams("arbitrary"),
        name="router",
    )(x, w_router_t, b_router, tri)


def _routing_tables(info, counts):
    cnt = counts[:, 0].astype(I32)
    tiles = (cnt + EXPERT_TILE - 1) // EXPERT_TILE
    tiles_end = jnp.cumsum(tiles)
    first_slot = (tiles_end - tiles) * EXPERT_TILE
    e0, e1 = info[0].astype(I32), info[1].astype(I32)
    dest0 = first_slot[e0] + info[4].astype(I32)
    dest1 = first_slot[e1] + info[5].astype(I32)
    n_valid = tiles_end[-1]
    tile = jnp.minimum(jnp.arange(N_EXPERT_TILES, dtype=I32), n_valid - 1)
    tile_expert = jnp.sum((tiles_end[None, :] <= tile[:, None]).astype(I32), axis=1)
    tile_expert = jnp.minimum(tile_expert, N_EXPERTS - 1)
    last_tile = jnp.where(tiles > 0, tiles_end - 1, -1).astype(I32)
    return dest0, dest1, tile, tile_expert, n_valid.reshape(1).astype(I32), last_tile


ROW_COPY_UNROLL = 8


def _row_copy(src_ref, src_row, dst_ref, dst_row, sem):
    return pltpu.make_async_copy(src_ref.at[pl.ds(src_row, 1)], dst_ref.at[pl.ds(dst_row, 1)], sem)


def _dispatch_body(dest0, dest1, last_tile, n_valid, x_ref, xs_ref, zero_scr, sem):
    step = pl.program_id(0)
    tm = x_ref.shape[0]

    @pl.when(step == 0)
    def _():
        zero_scr[...] = jnp.zeros_like(zero_scr)
        fill = lambda t: pltpu.make_async_copy(zero_scr, xs_ref.at[pl.ds(t * EXPERT_TILE, EXPERT_TILE)], sem)
        padded = [(last_tile[e], last_tile[e] >= 0) for e in range(N_EXPERTS)]
        unused = [(N_EXPERT_TILES - 1 - j, N_EXPERT_TILES - 1 - j >= n_valid[0]) for j in range(N_EXPERTS)]
        for t, needed in padded + unused:
            @pl.when(needed)
            def _():
                fill(t).start()
        for t, needed in padded + unused:
            @pl.when(needed)
            def _():
                fill(t).wait()

    base = step * tm

    def start(r, c):
        _row_copy(x_ref, r, xs_ref, dest0[base + r], sem).start()
        _row_copy(x_ref, r, xs_ref, dest1[base + r], sem).start()
        return c

    def wait(r, c):
        _row_copy(x_ref, r, xs_ref, dest0[base + r], sem).wait()
        _row_copy(x_ref, r, xs_ref, dest1[base + r], sem).wait()
        return c

    lax.fori_loop(0, tm, start, 0, unroll=ROW_COPY_UNROLL)
    lax.fori_loop(0, tm, wait, 0, unroll=ROW_COPY_UNROLL)


def _dispatch(x, dest0, dest1, last_tile, n_valid):
    return pl.pallas_call(
        _dispatch_body,
        grid_spec=pltpu.PrefetchScalarGridSpec(
            num_scalar_prefetch=4,
            grid=(T_ALL // ROW_TILE,),
            in_specs=[pl.BlockSpec((ROW_TILE, D_MODEL), lambda i, *_: (i, 0))],
            out_specs=pl.BlockSpec(memory_space=pl.ANY),
            scratch_shapes=[pltpu.VMEM((EXPERT_TILE, D_MODEL), F32), pltpu.SemaphoreType.DMA],
        ),
        out_shape=jax.ShapeDtypeStruct((N_SLOTS, D_MODEL), F32),
        compiler_params=_cparams("arbitrary"),
        name="moe_dispatch",
    )(dest0, dest1, last_tile, n_valid, x)


def _expert_body(tile, tile_expert, n_valid, x_ref, wg_ref, wu_ref, wd_ref, y_ref):
    used = pl.program_id(0) < n_valid[0]

    @pl.when(used)
    def _():
        x = x_ref[...].astype(BF16)
        g = jnp.dot(x, wg_ref[...], preferred_element_type=F32)
        u = jnp.dot(x, wu_ref[...], preferred_element_type=F32)
        hid = (g / (1.0 + jnp.exp(-g))) * u
        y_ref[...] = jnp.dot(hid.astype(BF16), wd_ref[...], preferred_element_type=F32)

    @pl.when(jnp.logical_not(used))
    def _():
        y_ref[...] = jnp.zeros_like(y_ref)


def _experts(xs, tile, tile_expert, n_valid, w_gate, w_up, w_down, layer):
    rows = pl.BlockSpec((EXPERT_TILE, D_MODEL), lambda i, tile, te, nv: (tile[i], 0))
    out_rows = pl.BlockSpec((EXPERT_TILE, D_MODEL), lambda i, tile, te, nv: (i, 0))
    weight = lambda k, n: pl.BlockSpec((None, None, k, n), lambda i, tile, te, nv: (layer, te[i], 0, 0))
    return pl.pallas_call(
        _expert_body,
        grid_spec=pltpu.PrefetchScalarGridSpec(
            num_scalar_prefetch=3,
            grid=(N_EXPERT_TILES,),
            in_specs=[rows, weight(D_MODEL, D_EXPERT), weight(D_MODEL, D_EXPERT), weight(D_EXPERT, D_MODEL)],
            out_specs=out_rows,
        ),
        out_shape=jax.ShapeDtypeStruct((N_SLOTS, D_MODEL), F32),
        compiler_params=_cparams("arbitrary"),
        name="moe_experts",
    )(tile, tile_expert, n_valid, xs, w_gate, w_up, w_down)


def _combine_ln_body(dest0, dest1, ys_ref, x_ref, w0_ref, w1_ref, g_ref, b_ref, o_ref, buf0, buf1, sem):
    tm = x_ref.shape[0]
    base = pl.program_id(0) * tm

    def start(r, c):
        _row_copy(ys_ref, dest0[base + r], buf0, r, sem).start()
        _row_copy(ys_ref, dest1[base + r], buf1, r, sem).start()
        return c

    def wait(r, c):
        _row_copy(ys_ref, dest0[base + r], buf0, r, sem).wait()
        _row_copy(ys_ref, dest1[base + r], buf1, r, sem).wait()
        return c

    lax.fori_loop(0, tm, start, 0, unroll=ROW_COPY_UNROLL)
    lax.fori_loop(0, tm, wait, 0, unroll=ROW_COPY_UNROLL)
    y = w0_ref[...] * buf0[...] + w1_ref[...] * buf1[...]
    o_ref[...] = _layer_norm(DEEPNORM_ALPHA * x_ref[...] + y, g_ref[...], b_ref[...])


def _combine_ln(ys, x, dest0, dest1, w0, w1, ln_g, ln_b, layer):
    row = lambda width: pl.BlockSpec((ROW_TILE, width), lambda i, *_: (i, 0))
    vec = pl.BlockSpec((None, 1, D_MODEL), lambda i, *_: (layer, 0, 0))
    return pl.pallas_call(
        _combine_ln_body,
        grid_spec=pltpu.PrefetchScalarGridSpec(
            num_scalar_prefetch=2,
            grid=(T_ALL // ROW_TILE,),
            in_specs=[pl.BlockSpec(memory_space=pl.ANY), row(D_MODEL), row(1), row(1), vec, vec],
            out_specs=row(D_MODEL),
            scratch_shapes=[pltpu.VMEM((ROW_TILE, D_MODEL), F32)] * 2 + [pltpu.SemaphoreType.DMA],
        ),
        out_shape=jax.ShapeDtypeStruct((T_ALL, D_MODEL), F32),
        compiler_params=_cparams("arbitrary"),
        name="moe_combine_ln2",
    )(dest0, dest1, ys, x, w0.reshape(T_ALL, 1), w1.reshape(T_ALL, 1), ln_g, ln_b)


def _moe_ln(x, router_params, w_gate, w_up, w_down, ln_g, ln_b, layer):
    info, counts = _router(x, *router_params)
    dest0, dest1, tile, tile_expert, n_valid, last_tile = _routing_tables(info, counts)
    xs = _dispatch(x, dest0, dest1, last_tile, n_valid)
    ys = _experts(xs, tile, tile_expert, n_valid, w_gate, w_up, w_down, layer)
    return _combine_ln(ys, x, dest0, dest1, info[2], info[3], ln_g, ln_b, layer)


def kernel(x_prompt, x_sample, cache_k, cache_v, state_ssm_re, state_ssm_im, w_in, w_out, g_attn_out, g_ssm_out, ssm_a_re, ssm_a_im, ssm_log_dt, ssm_b_re, ssm_b_im, ssm_c_re, ssm_c_im, ssm_d, w_glu, ln1_g, ln1_b, ln2_g, ln2_b, w_router, b_router, w_gate, w_up, w_down):
    x = jnp.concatenate([x_prompt.reshape(T_PROMPT, D_MODEL), x_sample.reshape(T_SAMPLE, D_MODEL)], axis=0)
    x = x.astype(F32)
    w_in, w_out, w_glu = w_in.astype(BF16), w_out.astype(BF16), w_glu.astype(BF16)
    w_gate, w_up, w_down = w_gate.astype(BF16), w_up.astype(BF16), w_down.astype(BF16)
    vecs = [v.astype(F32).reshape(DEPTH, 1, -1) for v in (g_attn_out, g_ssm_out, ssm_d, ln1_g, ln1_b, ln2_g, ln2_b)]
    g_attn_out, g_ssm_out, ssm_d, ln1_g, ln1_b, ln2_g, ln2_b = vecs
    router_params = (w_router.astype(F32).T, b_router.astype(F32).reshape(N_EXPERTS, 1),
                     (jnp.arange(ROUTER_TILE)[:, None] < jnp.arange(ROUTER_TILE)[None, :]).astype(BF16))
    to_rows_minor = lambda c: c.transpose(0, 1, 3, 4, 2).reshape(DEPTH, DEC_BATCH, ATTN_WIDTH, PAST_LEN)
    cache_kt, cache_vt = to_rows_minor(cache_k), to_rows_minor(cache_v)
    ssm_tables = _ssm_tables(ssm_a_re, ssm_a_im, ssm_log_dt, ssm_b_re, ssm_b_im, ssm_c_re, ssm_c_im)

    kp_out, vp_out, srp_out, sip_out = [], [], [], []
    ks_out, vs_out, srs_out, sis_out = [], [], [], []
    for l in range(DEPTH):
        z = _matmul(x, w_in, l, tm=2 * ROW_TILE, tn=ATTN_WIDTH, name="project_in")
        k_cols, v_cols = z[:, ATTN_WIDTH:2 * ATTN_WIDTH], z[:, 2 * ATTN_WIDTH:3 * ATTN_WIDTH]
        u = z[:, 3 * ATTN_WIDTH:]

        attn_p = _prompt_attention(z)
        attn_s = _sample_attention(z, cache_kt, cache_vt, l)

        y_p, h_p = _ssm_prompt(u[:T_PROMPT], _ssm_matrices(ssm_tables, l, SSM_CHUNK))
        y_s, h_s = _ssm_sample(u[T_PROMPT:], _state_to_pairs(state_ssm_re[l], state_ssm_im[l]),
                               _ssm_matrices(ssm_tables, l, DEC_SEQ))
        ssm = _glu(jnp.concatenate([y_p, y_s], axis=0), z, ssm_d, w_glu, l)

        x = _merge(attn_p, attn_s, ssm, x, g_attn_out, g_ssm_out, w_out, ln1_g, ln1_b, l)
        x = _moe_ln(x, router_params, w_gate, w_up, w_down, ln2_g, ln2_b, l)

        heads = lambda a, b: a.reshape(b, -1, N_HEADS, HEAD_DIM)
        window = min(2048, SEQ)
        kp_out.append(heads(k_cols[T_PROMPT - window:T_PROMPT], 1))
        vp_out.append(heads(v_cols[T_PROMPT - window:T_PROMPT], 1))
        ks_out.append(heads(k_cols[T_PROMPT:], DEC_BATCH))
        vs_out.append(heads(v_cols[T_PROMPT:], DEC_BATCH))
        re_p, im_p = _pairs_to_state(h_p[:, None, :])
        re_s, im_s = _pairs_to_state(h_s)
        srp_out.append(re_p)
        sip_out.append(im_p)
        srs_out.append(re_s)
        sis_out.append(im_s)
    return (x[:T_PROMPT].reshape(1, SEQ, D_MODEL), x[T_PROMPT:].reshape(DEC_BATCH, DEC_SEQ, D_MODEL),
            jnp.stack(kp_out), jnp.stack(vp_out), jnp.stack(srp_out), jnp.stack(sip_out),
            jnp.stack(ks_out), jnp.stack(vs_out), jnp.stack(srs_out), jnp.stack(sis_out))
```

```python
import functools
import math

import jax
import jax.numpy as jnp
from jax import lax
from jax.experimental import pallas as pl
from jax.experimental.pallas import tpu as pltpu

F32 = jnp.float32
BF16 = jnp.bfloat16
I32 = jnp.int32

D_MODEL = 2048
SEQ = 8192
DEPTH = 2
DEC_BATCH = 128
DEC_SEQ = 4
PAST_LEN = 2048
ATTN_WIDTH = D_MODEL // 2
SSM_WIDTH = D_MODEL - ATTN_WIDTH
HEAD_DIM = 64
N_HEADS = ATTN_WIDTH // HEAD_DIM
DILATED_BRANCHES = ((128, 1), (512, 4), (2048, 16))
LOOKBACK = 128
SSM_GROUP = 16
N_SSM_GROUPS = SSM_WIDTH // SSM_GROUP
N_PAIRS = N_SSM_GROUPS // 2
SSM_STATE = 64
PAIR_STATE = 4 * SSM_STATE
IN_COLS = 3 * ATTN_WIDTH + SSM_WIDTH
N_EXPERTS = 16
N_EXPERT_GROUPS = 4
EXPERTS_PER_GROUP = N_EXPERTS // N_EXPERT_GROUPS
D_EXPERT = D_MODEL // 2
DEEPNORM_ALPHA = (2.0 * DEPTH) ** 0.25
NORM_EPS = 1e-5

T_PROMPT = SEQ
T_SAMPLE = DEC_BATCH * DEC_SEQ
T_ALL = T_PROMPT + T_SAMPLE
SSM_CHUNK = 16
N_CHUNKS = T_PROMPT // SSM_CHUNK

V7X_VMEM_BYTES = 64 * 1024 * 1024
VMEM_LIMIT = V7X_VMEM_BYTES - 8 * 1024 * 1024

ROW_TILE = 256
MATMUL_ROW_TILE = 512
PROJECT_ROW_TILE = T_ALL // 8
EXPERT_TILE = 512
N_SLOTS = 2 * T_ALL + N_EXPERTS * EXPERT_TILE
N_EXPERT_TILES = N_SLOTS // EXPERT_TILE

ATTN_SCALE = 1.0 / math.sqrt(HEAD_DIM)
assert math.frexp(ATTN_SCALE)[0] == 0.5, "queries are pre-scaled before rounding, which needs a power of two"
SLOPE_LOG2_STEP = -8.0 / N_HEADS


def _cparams(*semantics):
    return pltpu.CompilerParams(dimension_semantics=semantics, vmem_limit_bytes=VMEM_LIMIT)


def _mm_body(x_ref, w_ref, o_ref):
    o_ref[...] = jnp.dot(x_ref[...].astype(BF16), w_ref[...], preferred_element_type=F32)


def _matmul(x, w, layer, *, tm, tn, name):
    m, k = x.shape
    n = w.shape[-1]
    return pl.pallas_call(
        _mm_body,
        grid=(n // tn, m // tm),
        in_specs=[pl.BlockSpec((tm, k), lambda j, i: (i, 0)),
                  pl.BlockSpec((None, k, tn), lambda j, i: (layer, 0, j))],
        out_specs=pl.BlockSpec((tm, tn), lambda j, i: (i, j)),
        out_shape=jax.ShapeDtypeStruct((m, n), F32),
        compiler_params=_cparams("parallel", "parallel"),
        name=name,
    )(x, w)


SUPER_BLOCK = 2048
PAIR_WIDTH = 2 * HEAD_DIM
BLOCK_UNROLL = 4


def _prompt_attn_body(q_ref, kp_ref, kc_ref, vp_ref, vc_ref, o_ref, k_all, v_all, o_br, lse_br):
    sb = pl.program_id(0)
    hp = pl.program_id(1)
    blk = LOOKBACK
    k_all[0:SUPER_BLOCK, :] = kp_ref[...]
    k_all[SUPER_BLOCK:2 * SUPER_BLOCK, :] = kc_ref[...]
    v_all[0:SUPER_BLOCK, :] = vp_ref[...]
    v_all[SUPER_BLOCK:2 * SUPER_BLOCK, :] = vc_ref[...]
    qi = lax.broadcasted_iota(I32, (blk, 2 * blk), 0)
    kj = lax.broadcasted_iota(I32, (blk, 2 * blk), 1)
    dist = qi + blk - kj
    in_band = (dist >= 0) & (dist <= LOOKBACK)
    low_lanes = lax.broadcasted_iota(I32, (blk, PAIR_WIDTH), 1) < HEAD_DIM
    head_lanes = (low_lanes, jnp.logical_not(low_lanes))
    slopes = [jnp.exp2(jnp.full((blk, 1), SLOPE_LOG2_STEP, F32) * (2 * hp + sub + 1).astype(F32))
              for sub in range(2)]
    nt = (((1,), (1,)), ((), ()))

    for bi, (_, dil) in enumerate(DILATED_BRANCHES):
        per_residue = SUPER_BLOCK // (blk * dil)
        bias = [slopes[sub] * (dist * dil).astype(F32) for sub in range(2)]
        stride = dil if dil > 1 else None

        def block(idx, carry, bi=bi, dil=dil, per_residue=per_residue, bias=bias, stride=stride):
            g = jnp.bitwise_and(idx, per_residue - 1)
            r = jnp.right_shift(idx, per_residue.bit_length() - 1)
            q0 = dil * blk * g + r
            kv0 = SUPER_BLOCK - dil * blk + q0
            if dil == 1:
                q0, kv0 = pl.multiple_of(q0, blk), pl.multiple_of(kv0, blk)
            q_rows = pl.ds(q0, blk, stride=stride)
            kv_rows = pl.ds(kv0, 2 * blk, stride=stride)
            first_key = jnp.where((sb == 0) & (g == 0), blk, 0)
            valid = in_band & (kj >= first_key)
            q2 = (q_ref[q_rows, :] * ATTN_SCALE).astype(BF16)
            k2 = k_all[kv_rows, :].astype(BF16)
            v2 = v_all[kv_rows, :].astype(BF16)
            outs, lses = [], []
            for sub in range(2):
                qm = jnp.where(head_lanes[sub], q2, jnp.zeros_like(q2))
                s = lax.dot_general(qm, k2, nt, preferred_element_type=F32)
                s = jnp.where(valid, s - bias[sub], -jnp.inf)
                m = jnp.max(s, axis=-1, keepdims=True)
                p = jnp.exp(s - m)
                l = jnp.sum(p, axis=-1, keepdims=True)
                o = jnp.dot(p.astype(BF16), v2, preferred_element_type=F32)
                outs.append(o / l)
                lses.append(jnp.broadcast_to(m + jnp.log(l), (blk, PAIR_WIDTH)))
            o_br[bi, q_rows, :] = jnp.where(low_lanes, outs[0], outs[1])
            lse_br[bi, q_rows, :] = jnp.where(low_lanes, lses[0], lses[1])
            return carry

        lax.fori_loop(0, dil * per_residue, block, 0, unroll=BLOCK_UNROLL)

    a, b, c = lse_br[0], lse_br[1], lse_br[2]
    mx = jnp.maximum(jnp.maximum(a, b), c)
    wa, wb, wc = jnp.exp(a - mx), jnp.exp(b - mx), jnp.exp(c - mx)
    o_ref[...] = (wa * o_br[0] + wb * o_br[1] + wc * o_br[2]) / (wa + wb + wc)


def _prompt_attention(z):
    n_pairs = ATTN_WIDTH // PAIR_WIDTH
    blk = (SUPER_BLOCK, PAIR_WIDTH)
    prev = lambda i: jnp.maximum(i - 1, 0)
    return pl.pallas_call(
        _prompt_attn_body,
        grid=(T_PROMPT // SUPER_BLOCK, n_pairs),
        in_specs=[pl.BlockSpec(blk, lambda i, h: (i, h)),
                  pl.BlockSpec(blk, lambda i, h: (prev(i), n_pairs + h)),
                  pl.BlockSpec(blk, lambda i, h: (i, n_pairs + h)),
                  pl.BlockSpec(blk, lambda i, h: (prev(i), 2 * n_pairs + h)),
                  pl.BlockSpec(blk, lambda i, h: (i, 2 * n_pairs + h))],
        out_specs=pl.BlockSpec(blk, lambda i, h: (i, h)),
        out_shape=jax.ShapeDtypeStruct((T_PROMPT, ATTN_WIDTH), F32),
        scratch_shapes=[pltpu.VMEM((2 * SUPER_BLOCK, PAIR_WIDTH), F32)] * 2
                       + [pltpu.VMEM((len(DILATED_BRANCHES), SUPER_BLOCK, PAIR_WIDTH), F32)] * 2,
        compiler_params=_cparams("parallel", "parallel"),
        name="prompt_attention",
    )(z, z, z, z, z)


NEW_ROWS_PAD = 16
OUT_ROWS_PAD = 8


def _sample_attn_body(zs_ref, kt_ref, vt_ref, o_ref, k_scr, v_scr):
    s_len = DEC_SEQ
    rows = s_len * N_HEADS
    head_shift = N_HEADS.bit_length() - 1
    zs = zs_ref[...]
    q = zs[:, 0:ATTN_WIDTH]
    k_new = zs[:, ATTN_WIDTH:2 * ATTN_WIDTH].astype(BF16)
    v_new = zs[:, 2 * ATTN_WIDTH:3 * ATTN_WIDTH].astype(BF16)
    k_scr[...] = kt_ref[...].astype(BF16)
    v_scr[...] = vt_ref[...].astype(BF16)

    lane_head = jnp.right_shift(lax.broadcasted_iota(I32, (N_HEADS, ATTN_WIDTH), 1), HEAD_DIM.bit_length() - 1)
    head_mask = lane_head == lax.broadcasted_iota(I32, (N_HEADS, ATTN_WIDTH), 0)
    qbd = jnp.concatenate(
        [jnp.where(head_mask, jnp.broadcast_to(q[s:s + 1, :], (N_HEADS, ATTN_WIDTH)), 0.0)
         for s in range(s_len)], axis=0).astype(BF16)
    row_head = jnp.bitwise_and(lax.broadcasted_iota(I32, (rows, 1), 0), N_HEADS - 1)
    slope = jnp.exp2((row_head + 1).astype(F32) * SLOPE_LOG2_STEP)
    nt = (((1,), (1,)), ((), ()))

    sc = jnp.dot(qbd, k_scr[...], preferred_element_type=F32)
    dist = (PAST_LEN + jnp.right_shift(lax.broadcasted_iota(I32, (rows, PAST_LEN), 0), head_shift)
            - lax.broadcasted_iota(I32, (rows, PAST_LEN), 1))
    mult = jnp.zeros((rows, PAST_LEN), F32)
    for window, dil in DILATED_BRANCHES:
        mult = mult + jnp.where((dist <= window) & (jnp.bitwise_and(dist, dil - 1) == 0), 1.0, 0.0)
    sc = jnp.where(mult > 0, sc * ATTN_SCALE - slope * dist.astype(F32), -jnp.inf)

    sn = lax.dot_general(qbd, k_new, nt, preferred_element_type=F32)
    col_n = lax.broadcasted_iota(I32, (rows, NEW_ROWS_PAD), 1)
    dist_n = jnp.right_shift(lax.broadcasted_iota(I32, (rows, NEW_ROWS_PAD), 0), head_shift) - col_n
    mult_n = jnp.where((dist_n >= 0) & (col_n < s_len),
                       jnp.where(dist_n == 0, float(len(DILATED_BRANCHES)), 1.0), 0.0)
    sn = jnp.where(mult_n > 0, sn * ATTN_SCALE - slope * dist_n.astype(F32), -jnp.inf)

    m = jnp.maximum(jnp.max(sc, axis=-1, keepdims=True), jnp.max(sn, axis=-1, keepdims=True))
    pc = mult * jnp.exp(sc - m)
    pn = mult_n * jnp.exp(sn - m)
    l = jnp.sum(pc, axis=-1, keepdims=True) + jnp.sum(pn, axis=-1, keepdims=True)
    acc = (lax.dot_general(pc.astype(BF16), v_scr[...], nt, preferred_element_type=F32)
           + jnp.dot(pn.astype(BF16), v_new, preferred_element_type=F32)) / l
    o_ref[...] = jnp.zeros_like(o_ref)
    for s in range(s_len):
        o_ref[s:s + 1, :] = jnp.sum(jnp.where(head_mask, acc[s * N_HEADS:(s + 1) * N_HEADS, :], 0.0),
                                    axis=0, keepdims=True)


def _sample_attention(z, cache_kt, cache_vt, layer):
    zs = jnp.pad(z[T_PROMPT:].reshape(DEC_BATCH, DEC_SEQ, IN_COLS), ((0, 0), (0, NEW_ROWS_PAD - DEC_SEQ), (0, 0)))
    cache_spec = pl.BlockSpec((None, None, ATTN_WIDTH, PAST_LEN), lambda b: (layer, b, 0, 0))
    out = pl.pallas_call(
        _sample_attn_body,
        grid=(DEC_BATCH,),
        in_specs=[pl.BlockSpec((None, NEW_ROWS_PAD, IN_COLS), lambda b: (b, 0, 0)), cache_spec, cache_spec],
        out_specs=pl.BlockSpec((None, OUT_ROWS_PAD, ATTN_WIDTH), lambda b: (b, 0, 0)),
        out_shape=jax.ShapeDtypeStruct((DEC_BATCH, OUT_ROWS_PAD, ATTN_WIDTH), F32),
        scratch_shapes=[pltpu.VMEM((ATTN_WIDTH, PAST_LEN), BF16)] * 2,
        compiler_params=_cparams("parallel"),
        name="sample_attention",
    )(zs, cache_kt, cache_vt)
    return out[:, :DEC_SEQ].reshape(T_SAMPLE, ATTN_WIDTH)


def _ssm_tables(a_re, a_im, log_dt, b_re, b_im, c_re, c_im):
    depth = a_re.shape[0]
    dt = jnp.exp(log_dt.astype(F32))[:, None, :, None]
    a_re, a_im = a_re.astype(F32)[:, None], a_im.astype(F32)[:, None]

    def powers(ks):
        e = ks.astype(F32)[None, :, None, None]
        mag = jnp.exp(e * dt * a_re)
        return mag * jnp.cos(e * dt * a_im), mag * jnp.sin(e * dt * a_im)

    pos_re, pos_im = powers(jnp.arange(SSM_CHUNK + 1))
    neg_re, neg_im = powers(-jnp.arange(SSM_CHUNK))
    den = (a_re * a_re + a_im * a_im)[:, 0]
    n_re, n_im = pos_re[:, 1] - 1.0, pos_im[:, 1]
    f_re = (n_re * a_re[:, 0] + n_im * a_im[:, 0]) / den
    f_im = (n_im * a_re[:, 0] - n_re * a_im[:, 0]) / den
    b_re, b_im = b_re.astype(F32), b_im.astype(F32)
    bb_re = (f_re[..., None] * b_re - f_im[..., None] * b_im).transpose(0, 1, 3, 2)
    bb_im = (f_re[..., None] * b_im + f_im[..., None] * b_re).transpose(0, 1, 3, 2)

    def lanes(re, im):
        x = jnp.stack([re, im], axis=1)
        return x.reshape(depth, 2, -1, N_PAIRS, 2 * SSM_STATE).transpose(0, 3, 1, 2, 4)

    def own_lanes(re, im):
        x = jnp.stack([re, im], axis=1).reshape(depth, 2, N_PAIRS, 2, SSM_GROUP, SSM_STATE)
        x = jnp.einsum('drjgcn,gh->djrgchn', x, jnp.eye(2, dtype=F32))
        return x.reshape(depth, N_PAIRS, 2, 2, SSM_GROUP, 2 * SSM_STATE)

    return (lanes(pos_re, pos_im), lanes(neg_re, neg_im), own_lanes(bb_re, bb_im),
            own_lanes(c_re.astype(F32), c_im.astype(F32)))


def _ssm_build_body(pos_ref, neg_ref, bb_ref, cc_ref, w_ref, p_ref, rt_ref, x_scr, y_scr, *, chunk):
    per_token = 2 * SSM_GROUP
    cmul = lambda ar, ai, br, bi: (ar * br - ai * bi, ar * bi + ai * br)
    power = lambda ref, k: (ref[0, k:k + 1, :], ref[1, k:k + 1, :])
    for gg in range(2):
        b_r, b_i, c_r, c_i = bb_ref[0, gg], bb_ref[1, gg], cc_ref[0, gg], cc_ref[1, gg]
        for s in range(chunk):
            sl = slice(s * per_token + gg * SSM_GROUP, s * per_token + (gg + 1) * SSM_GROUP)
            p_r, p_i = cmul(*power(pos_ref, chunk - 1 - s), b_r, b_i)
            p_ref[sl, :] = jnp.concatenate([p_r, p_i], axis=1).astype(BF16)
            x_r, x_i = cmul(*power(neg_ref, s), b_r, b_i)
            x_scr[sl, :] = jnp.concatenate([x_r, x_i], axis=1)
            y_r, y_i = cmul(*power(pos_ref, s), c_r, c_i)
            y_scr[sl, :] = jnp.concatenate([y_r, -y_i], axis=1)
            r_r, r_i = cmul(*power(pos_ref, s + 1), c_r, c_i)
            rt_ref[sl, :] = jnp.concatenate([r_r, -r_i], axis=1).astype(BF16)
    w = lax.dot_general(x_scr[...], y_scr[...], (((1,), (1,)), ((), ())),
                        precision=lax.Precision.HIGHEST, preferred_element_type=F32)
    tc = chunk * per_token
    tok = lambda dim: jnp.right_shift(lax.broadcasted_iota(I32, (tc, tc), dim), per_token.bit_length() - 1)
    w_ref[...] = jnp.where(tok(1) >= tok(0), w, 0.0).astype(BF16)


def _ssm_matrices(tables, layer, chunk):
    pos, neg, bb, cc = tables
    tc = 2 * chunk * SSM_GROUP
    table = lambda t: pl.BlockSpec((None, None) + t.shape[2:], lambda j: (layer, j) + (0,) * (t.ndim - 2))
    out = lambda width: pl.BlockSpec((None, tc, width), lambda j: (j, 0, 0))
    w, p, rt = pl.pallas_call(
        functools.partial(_ssm_build_body, chunk=chunk),
        grid=(N_PAIRS,),
        in_specs=[table(pos), table(neg), table(bb), table(cc)],
        out_specs=[out(tc), out(PAIR_STATE), out(PAIR_STATE)],
        out_shape=[jax.ShapeDtypeStruct((N_PAIRS, tc, tc), BF16),
                   jax.ShapeDtypeStruct((N_PAIRS, tc, PAIR_STATE), BF16),
                   jax.ShapeDtypeStruct((N_PAIRS, tc, PAIR_STATE), BF16)],
        scratch_shapes=[pltpu.VMEM((tc, PAIR_STATE), F32)] * 2,
        compiler_params=_cparams("parallel"),
        name="ssm_build",
    )(pos, neg, bb, cc)
    at_re, at_im = pos[layer, :, 0, chunk], pos[layer, :, 1, chunk]
    m1 = jnp.concatenate([at_re, at_re], axis=-1)
    m2 = jnp.concatenate([-at_im, at_im], axis=-1)
    return w, p, rt, m1, m2


def _swap_halves(h, width):
    parts = []
    for c in range(h.shape[-1] // (2 * width)):
        parts += [h[..., (2 * c + 1) * width:(2 * c + 2) * width], h[..., 2 * c * width:(2 * c + 1) * width]]
    return jnp.concatenate(parts, axis=-1)


def _ssm_z_body(u_ref, p_ref, z_ref):
    z_ref[...] = jnp.dot(u_ref[...].astype(BF16), p_ref[...], preferred_element_type=F32)


def _ssm_scan_body(z_ref, m1_ref, m2_ref, hin_ref, hlast_ref, h_scr):
    @pl.when(pl.program_id(0) == 0)
    def _():
        h_scr[...] = jnp.zeros_like(h_scr)

    m1, m2 = m1_ref[...], m2_ref[...]

    def step(k, h):
        hin_ref[k] = h
        return m1 * h + m2 * _swap_halves(h, 2 * SSM_STATE) + z_ref[k]

    h = lax.fori_loop(0, z_ref.shape[0], step, h_scr[...])
    h_scr[...] = h
    hlast_ref[...] = h


_NT = (((1,), (1,)), ((), ()))


def _ssm_y_body(u_ref, w_ref, hin_ref, rt_ref, y_ref):
    y_ref[...] = (jnp.dot(u_ref[...].astype(BF16), w_ref[...], preferred_element_type=F32)
                  + lax.dot_general(hin_ref[...].astype(BF16), rt_ref[...], _NT, preferred_element_type=F32))


def _to_pairs(u, n_chunks, chunk):
    u = u.reshape(n_chunks, chunk, N_PAIRS, 2 * SSM_GROUP).transpose(2, 0, 1, 3)
    return u.reshape(N_PAIRS, n_chunks, 2 * chunk * SSM_GROUP)


def _from_pairs(y, n_chunks, chunk):
    y = y.reshape(N_PAIRS, n_chunks, chunk, 2 * SSM_GROUP).transpose(1, 2, 0, 3)
    return y.reshape(n_chunks * chunk, SSM_WIDTH)


def _ssm_prompt(u, mats):
    w, p, rt, m1, m2 = mats
    tc = SSM_CHUNK * 2 * SSM_GROUP
    u_pairs = _to_pairs(u, N_CHUNKS, SSM_CHUNK)
    sub = 8
    state_w = N_PAIRS * PAIR_STATE // sub
    z = pl.pallas_call(
        _ssm_z_body,
        grid=(N_PAIRS,),
        in_specs=[pl.BlockSpec((None, N_CHUNKS, tc), lambda j: (j, 0, 0)),
                  pl.BlockSpec((None, tc, PAIR_STATE), lambda j: (j, 0, 0))],
        out_specs=pl.BlockSpec((N_CHUNKS, PAIR_STATE), lambda j: (0, j)),
        out_shape=jax.ShapeDtypeStruct((N_CHUNKS, N_PAIRS * PAIR_STATE), F32),
        compiler_params=_cparams("parallel"),
        name="ssm_chunk_state",
    )(u_pairs, p)
    scan_rows = 128
    hin, hlast = pl.pallas_call(
        _ssm_scan_body,
        grid=(N_CHUNKS // scan_rows,),
        in_specs=[pl.BlockSpec((scan_rows, sub, state_w), lambda i: (i, 0, 0)),
                  pl.BlockSpec((sub, state_w), lambda i: (0, 0)),
                  pl.BlockSpec((sub, state_w), lambda i: (0, 0))],
        out_specs=[pl.BlockSpec((scan_rows, sub, state_w), lambda i: (i, 0, 0)),
                   pl.BlockSpec((sub, state_w), lambda i: (0, 0))],
        out_shape=[jax.ShapeDtypeStruct((N_CHUNKS, sub, state_w), F32),
                   jax.ShapeDtypeStruct((sub, state_w), F32)],
        scratch_shapes=[pltpu.VMEM((sub, state_w), F32)],
        compiler_params=_cparams("arbitrary"),
        name="ssm_scan",
    )(z.reshape(N_CHUNKS, sub, state_w), m1.reshape(sub, state_w), m2.reshape(sub, state_w))
    y = pl.pallas_call(
        _ssm_y_body,
        grid=(N_PAIRS,),
        in_specs=[pl.BlockSpec((None, N_CHUNKS, tc), lambda j: (j, 0, 0)),
                  pl.BlockSpec((None, tc, tc), lambda j: (j, 0, 0)),
                  pl.BlockSpec((N_CHUNKS, PAIR_STATE), lambda j: (0, j)),
                  pl.BlockSpec((None, tc, PAIR_STATE), lambda j: (j, 0, 0))],
        out_specs=pl.BlockSpec((None, N_CHUNKS, tc), lambda j: (j, 0, 0)),
        out_shape=jax.ShapeDtypeStruct((N_PAIRS, N_CHUNKS, tc), F32),
        compiler_params=_cparams("parallel"),
        name="ssm_chunk_output",
    )(u_pairs, w, hin.reshape(N_CHUNKS, N_PAIRS * PAIR_STATE), rt)
    return _from_pairs(y, N_CHUNKS, SSM_CHUNK), hlast.reshape(N_PAIRS, PAIR_STATE)


def _ssm_sample_body(u_ref, h0_ref, p_ref, w_ref, rt_ref, m1_ref, m2_ref, y_ref, hl_ref):
    u = u_ref[...].astype(BF16)
    h0 = h0_ref[...]
    y_ref[...] = (jnp.dot(u, w_ref[...], preferred_element_type=F32)
                  + lax.dot_general(h0.astype(BF16), rt_ref[...], _NT, preferred_element_type=F32))
    hl_ref[...] = (m1_ref[...] * h0 + m2_ref[...] * _swap_halves(h0, 2 * SSM_STATE)
                   + jnp.dot(u, p_ref[...], preferred_element_type=F32))


def _ssm_sample(u, h0, mats):
    w, p, rt, m1, m2 = mats
    tc = DEC_SEQ * 2 * SSM_GROUP
    u_pairs = _to_pairs(u, DEC_BATCH, DEC_SEQ)
    per_pair = lambda *shape: pl.BlockSpec((None,) + shape, lambda j: (j,) + (0,) * len(shape))
    y, hl = pl.pallas_call(
        _ssm_sample_body,
        grid=(N_PAIRS,),
        in_specs=[per_pair(DEC_BATCH, tc), per_pair(DEC_BATCH, PAIR_STATE), per_pair(tc, PAIR_STATE),
                  per_pair(tc, tc), per_pair(tc, PAIR_STATE), per_pair(1, PAIR_STATE), per_pair(1, PAIR_STATE)],
        out_specs=[per_pair(DEC_BATCH, tc), per_pair(DEC_BATCH, PAIR_STATE)],
        out_shape=[jax.ShapeDtypeStruct((N_PAIRS, DEC_BATCH, tc), F32),
                   jax.ShapeDtypeStruct((N_PAIRS, DEC_BATCH, PAIR_STATE), F32)],
        compiler_params=_cparams("parallel"),
        name="ssm_sample",
    )(u_pairs, h0, p, w, rt, m1[:, None, :], m2[:, None, :])
    return _from_pairs(y, DEC_BATCH, DEC_SEQ), hl


def _state_to_pairs(re, im):
    b = re.shape[0]
    f = lambda x: x.astype(F32).reshape(b, N_PAIRS, 2 * SSM_STATE).transpose(1, 0, 2)
    return jnp.concatenate([f(re), f(im)], axis=-1)


def _pairs_to_state(h):
    b = h.shape[1]
    f = lambda x: x.transpose(1, 0, 2).reshape(b, N_SSM_GROUPS, SSM_STATE)
    return f(h[..., :2 * SSM_STATE]), f(h[..., 2 * SSM_STATE:])


def _glu_body(y_ref, u_ref, d_ref, w_ref, o_ref):
    y = jax.nn.gelu(y_ref[...] + d_ref[...] * u_ref[...])
    zz = jnp.dot(y.astype(BF16), w_ref[...], preferred_element_type=F32)
    val, gate = zz[:, :SSM_WIDTH], zz[:, SSM_WIDTH:]
    o_ref[...] = val / (1.0 + jnp.exp(-gate))


def _glu(y, z, d_skip, w_glu, layer):
    tm = MATMUL_ROW_TILE
    return pl.pallas_call(
        _glu_body,
        grid=(T_ALL // tm,),
        in_specs=[pl.BlockSpec((tm, SSM_WIDTH), lambda i: (i, 0)),
                  pl.BlockSpec((tm, SSM_WIDTH), lambda i: (i, 3 * ATTN_WIDTH // SSM_WIDTH)),
                  pl.BlockSpec((None, 1, SSM_WIDTH), lambda i: (layer, 0, 0)),
                  pl.BlockSpec((None, SSM_WIDTH, 2 * SSM_WIDTH), lambda i: (layer, 0, 0))],
        out_specs=pl.BlockSpec((tm, SSM_WIDTH), lambda i: (i, 0)),
        out_shape=jax.ShapeDtypeStruct((T_ALL, SSM_WIDTH), F32),
        compiler_params=_cparams("parallel"),
        name="ssm_glu",
    )(y, z, d_skip, w_glu)


def _layer_norm(x, g, b):
    mu = jnp.mean(x, axis=-1, keepdims=True)
    xc = x - mu
    var = jnp.mean(xc * xc, axis=-1, keepdims=True)
    return xc * lax.rsqrt(var + NORM_EPS) * g + b


def _rms_norm(x, g):
    return x * lax.rsqrt(jnp.mean(x * x, axis=-1, keepdims=True) + NORM_EPS) * g


def _merge_body(ap_ref, as_ref, ssm_ref, x_ref, ga_ref, gs_ref, wa_ref, ws_ref, g_ref, b_ref, o_ref):
    is_prompt = pl.program_id(0) < T_PROMPT // MATMUL_ROW_TILE
    attn = jnp.where(is_prompt, ap_ref[...], as_ref[...])
    an = _rms_norm(attn, ga_ref[...]).astype(BF16)
    sn = _rms_norm(ssm_ref[...], gs_ref[...]).astype(BF16)
    mix = (jnp.dot(an, wa_ref[...], preferred_element_type=F32)
           + jnp.dot(sn, ws_ref[...], preferred_element_type=F32))
    o_ref[...] = _layer_norm(DEEPNORM_ALPHA * x_ref[...] + mix, g_ref[...], b_ref[...])


def _merge(attn_p, attn_s, ssm, x, g_attn, g_ssm, w_out, ln_g, ln_b, layer):
    tm = MATMUL_ROW_TILE
    n_p = T_PROMPT // tm
    row = lambda width: pl.BlockSpec((tm, width), lambda i: (i, 0))
    vec = lambda width: pl.BlockSpec((None, 1, width), lambda i: (layer, 0, 0))
    half = lambda which: pl.BlockSpec((None, ATTN_WIDTH, D_MODEL), lambda i: (layer, which, 0))
    return pl.pallas_call(
        _merge_body,
        grid=(T_ALL // tm,),
        in_specs=[pl.BlockSpec((tm, ATTN_WIDTH), lambda i: (jnp.minimum(i, n_p - 1), 0)),
                  pl.BlockSpec((tm, ATTN_WIDTH), lambda i: (jnp.maximum(i - n_p, 0), 0)),
                  row(SSM_WIDTH), row(D_MODEL), vec(ATTN_WIDTH), vec(SSM_WIDTH), half(0), half(1),
                  vec(D_MODEL), vec(D_MODEL)],
        out_specs=row(D_MODEL),
        out_shape=jax.ShapeDtypeStruct((T_ALL, D_MODEL), F32),
        compiler_params=_cparams("parallel"),
        name="merge_project_ln1",
    )(attn_p, attn_s, ssm, x, g_attn, g_ssm, w_out, w_out, ln_g, ln_b)


ROUTER_TILE = 512
INFO_ROWS = 8


def _top2_of4(a, b, c, d):
    m1, n1 = jnp.maximum(a, b), jnp.minimum(a, b)
    m2, n2 = jnp.maximum(c, d), jnp.minimum(c, d)
    return jnp.maximum(m1, m2) + jnp.maximum(jnp.minimum(m1, m2), jnp.maximum(n1, n2))


def _router_body(x_ref, wr_ref, br_ref, tri_ref, info_ref, cnt_ref, carry):
    @pl.when(pl.program_id(0) == 0)
    def _():
        carry[...] = jnp.zeros_like(carry)

    tm = x_ref.shape[0]
    logits = lax.dot_general(wr_ref[...], x_ref[...], (((1,), (1,)), ((), ())),
                             precision=lax.Precision.HIGHEST, preferred_element_type=F32)
    ex = jnp.exp(logits - jnp.max(logits, axis=0, keepdims=True))
    scores = ex / jnp.sum(ex, axis=0, keepdims=True)
    sel = scores + br_ref[...]
    row = lambda a, i: a[i:i + 1, :]
    epg = EXPERTS_PER_GROUP
    grp = [_top2_of4(*[row(sel, epg * g + i) for i in range(epg)]) for g in range(N_EXPERT_GROUPS)]
    best, best_v = jnp.zeros((1, tm), I32), grp[0]
    for g in range(1, N_EXPERT_GROUPS):
        upd = grp[g] > best_v
        best, best_v = jnp.where(upd, g, best), jnp.where(upd, grp[g], best_v)

    def in_group(a, i):
        out = row(a, i)
        for g in range(1, N_EXPERT_GROUPS):
            out = jnp.where(best == g, row(a, epg * g + i), out)
        return out

    v = [in_group(sel, i) for i in range(epg)]
    sc = [in_group(scores, i) for i in range(epg)]
    i0, v0, s0 = jnp.zeros((1, tm), I32), v[0], sc[0]
    for i in range(1, epg):
        upd = v[i] > v0
        i0, v0, s0 = jnp.where(upd, i, i0), jnp.where(upd, v[i], v0), jnp.where(upd, sc[i], s0)
    i1, v1, s1 = jnp.zeros((1, tm), I32), jnp.full((1, tm), -jnp.inf, F32), jnp.zeros((1, tm), F32)
    for i in range(epg):
        upd = (i0 != i) & (v[i] > v1)
        i1, v1, s1 = jnp.where(upd, i, i1), jnp.where(upd, v[i], v1), jnp.where(upd, sc[i], s1)
    e0, e1 = best * epg + i0, best * epg + i1
    wsum = s0 + s1

    eidx = lax.broadcasted_iota(I32, (N_EXPERTS, tm), 0)
    hit0, hit1 = eidx == e0, eidx == e1
    onehot = jnp.where(hit0 | hit1, 1.0, 0.0)
    before = jnp.dot(onehot.astype(BF16), tri_ref[...], preferred_element_type=F32) + carry[...]
    rank0 = jnp.sum(jnp.where(hit0, before, 0.0), axis=0, keepdims=True)
    rank1 = jnp.sum(jnp.where(hit1, before, 0.0), axis=0, keepdims=True)
    total = carry[...] + jnp.sum(onehot, axis=1, keepdims=True)
    carry[...] = total
    cnt_ref[...] = jnp.broadcast_to(total, cnt_ref.shape)
    info_ref[...] = jnp.zeros_like(info_ref)
    for i, val in enumerate((e0.astype(F32), e1.astype(F32), s0 / wsum, s1 / wsum, rank0, rank1)):
        info_ref[i:i + 1, :] = val


def _router(x, w_router_t, b_router, tri):
    tm = ROUTER_TILE
    return pl.pallas_call(
        _router_body,
        grid=(T_ALL // tm,),
        in_specs=[pl.BlockSpec((tm, D_MODEL), lambda i: (i, 0)),
                  pl.BlockSpec((N_EXPERTS, D_MODEL), lambda i: (0, 0)),
                  pl.BlockSpec((N_EXPERTS, 1), lambda i: (0, 0)),
                  pl.BlockSpec((tm, tm), lambda i: (0, 0))],
        out_specs=[pl.BlockSpec((INFO_ROWS, tm), lambda i: (0, i)),
                   pl.BlockSpec((N_EXPERTS, 128), lambda i: (0, 0))],
        out_shape=[jax.ShapeDtypeStruct((INFO_ROWS, T_ALL), F32),
                   jax.ShapeDtypeStruct((N_EXPERTS, 128), F32)],
        scratch_shapes=[pltpu.VMEM((N_EXPERTS, 1), F32)],
        compiler_params=_cparams("arbitrary"),
        name="router",
    )(x, w_router_t, b_router, tri)


def _routing_tables(info, counts):
    cnt = counts[:, 0].astype(I32)
    tiles = (cnt + EXPERT_TILE - 1) // EXPERT_TILE
    tiles_end = jnp.cumsum(tiles)
    first_slot = (tiles_end - tiles) * EXPERT_TILE
    e0, e1 = info[0].astype(I32), info[1].astype(I32)
    dest0 = first_slot[e0] + info[4].astype(I32)
    dest1 = first_slot[e1] + info[5].astype(I32)
    n_valid = tiles_end[-1]
    tile = jnp.minimum(jnp.arange(N_EXPERT_TILES, dtype=I32), n_valid - 1)
    tile_expert = jnp.sum((tiles_end[None, :] <= tile[:, None]).astype(I32), axis=1)
    tile_expert = jnp.minimum(tile_expert, N_EXPERTS - 1)
    last_tile = jnp.where(tiles > 0, tiles_end - 1, -1).astype(I32)
    return dest0, dest1, tile, tile_expert, n_valid.reshape(1).astype(I32), last_tile


ROW_COPY_UNROLL = 8


def _row_copy(src_ref, src_row, dst_ref, dst_row, sem):
    return pltpu.make_async_copy(src_ref.at[pl.ds(src_row, 1)], dst_ref.at[pl.ds(dst_row, 1)], sem)


def _dispatch_body(dest0, dest1, last_tile, n_valid, x_ref, xs_ref, zero_scr, sem):
    step = pl.program_id(0)
    tm = x_ref.shape[0]

    @pl.when(step == 0)
    def _():
        zero_scr[...] = jnp.zeros_like(zero_scr)
        fill = lambda t: pltpu.make_async_copy(zero_scr, xs_ref.at[pl.ds(t * EXPERT_TILE, EXPERT_TILE)], sem)
        padded = [(last_tile[e], last_tile[e] >= 0) for e in range(N_EXPERTS)]
        unused = [(N_EXPERT_TILES - 1 - j, N_EXPERT_TILES - 1 - j >= n_valid[0]) for j in range(N_EXPERTS)]
        for t, needed in padded + unused:
            @pl.when(needed)
            def _():
                fill(t).start()
        for t, needed in padded + unused:
            @pl.when(needed)
            def _():
                fill(t).wait()

    base = step * tm

    def start(r, c):
        _row_copy(x_ref, r, xs_ref, dest0[base + r], sem).start()
        _row_copy(x_ref, r, xs_ref, dest1[base + r], sem).start()
        return c

    def wait(r, c):
        _row_copy(x_ref, r, xs_ref, dest0[base + r], sem).wait()
        _row_copy(x_ref, r, xs_ref, dest1[base + r], sem).wait()
        return c

    lax.fori_loop(0, tm, start, 0, unroll=ROW_COPY_UNROLL)
    lax.fori_loop(0, tm, wait, 0, unroll=ROW_COPY_UNROLL)


def _dispatch(x, dest0, dest1, last_tile, n_valid):
    return pl.pallas_call(
        _dispatch_body,
        grid_spec=pltpu.PrefetchScalarGridSpec(
            num_scalar_prefetch=4,
            grid=(T_ALL // ROW_TILE,),
            in_specs=[pl.BlockSpec((ROW_TILE, D_MODEL), lambda i, *_: (i, 0))],
            out_specs=pl.BlockSpec(memory_space=pl.ANY),
            scratch_shapes=[pltpu.VMEM((EXPERT_TILE, D_MODEL), F32), pltpu.SemaphoreType.DMA],
        ),
        out_shape=jax.ShapeDtypeStruct((N_SLOTS, D_MODEL), F32),
        compiler_params=_cparams("arbitrary"),
        name="moe_dispatch",
    )(dest0, dest1, last_tile, n_valid, x)


def _expert_body(tile, tile_expert, n_valid, x_ref, wg_ref, wu_ref, wd_ref, y_ref):
    used = pl.program_id(0) < n_valid[0]

    @pl.when(used)
    def _():
        x = x_ref[...].astype(BF16)
        g = jnp.dot(x, wg_ref[...], preferred_element_type=F32)
        u = jnp.dot(x, wu_ref[...], preferred_element_type=F32)
        hid = (g / (1.0 + jnp.exp(-g))) * u
        y_ref[...] = jnp.dot(hid.astype(BF16), wd_ref[...], preferred_element_type=F32)

    @pl.when(jnp.logical_not(used))
    def _():
        y_ref[...] = jnp.zeros_like(y_ref)


def _experts(xs, tile, tile_expert, n_valid, w_gate, w_up, w_down, layer):
    rows = pl.BlockSpec((EXPERT_TILE, D_MODEL), lambda i, tile, te, nv: (tile[i], 0))
    out_rows = pl.BlockSpec((EXPERT_TILE, D_MODEL), lambda i, tile, te, nv: (i, 0))
    weight = lambda k, n: pl.BlockSpec((None, None, k, n), lambda i, tile, te, nv: (layer, te[i], 0, 0))
    return pl.pallas_call(
        _expert_body,
        grid_spec=pltpu.PrefetchScalarGridSpec(
            num_scalar_prefetch=3,
            grid=(N_EXPERT_TILES,),
            in_specs=[rows, weight(D_MODEL, D_EXPERT), weight(D_MODEL, D_EXPERT), weight(D_EXPERT, D_MODEL)],
            out_specs=out_rows,
        ),
        out_shape=jax.ShapeDtypeStruct((N_SLOTS, D_MODEL), F32),
        compiler_params=_cparams("arbitrary"),
        name="moe_experts",
    )(tile, tile_expert, n_valid, xs, w_gate, w_up, w_down)


def _combine_ln_body(dest0, dest1, ys_ref, x_ref, w0_ref, w1_ref, g_ref, b_ref, o_ref, buf0, buf1, sem):
    tm = x_ref.shape[0]
    base = pl.program_id(0) * tm

    def start(r, c):
        _row_copy(ys_ref, dest0[base + r], buf0, r, sem).start()
        _row_copy(ys_ref, dest1[base + r], buf1, r, sem).start()
        return c

    def wait(r, c):
        _row_copy(ys_ref, dest0[base + r], buf0, r, sem).wait()
        _row_copy(ys_ref, dest1[base + r], buf1, r, sem).wait()
        return c

    lax.fori_loop(0, tm, start, 0, unroll=ROW_COPY_UNROLL)
    lax.fori_loop(0, tm, wait, 0, unroll=ROW_COPY_UNROLL)
    y = w0_ref[...] * buf0[...] + w1_ref[...] * buf1[...]
    o_ref[...] = _layer_norm(DEEPNORM_ALPHA * x_ref[...] + y, g_ref[...], b_ref[...])


def _combine_ln(ys, x, dest0, dest1, w0, w1, ln_g, ln_b, layer):
    row = lambda width: pl.BlockSpec((ROW_TILE, width), lambda i, *_: (i, 0))
    vec = pl.BlockSpec((None, 1, D_MODEL), lambda i, *_: (layer, 0, 0))
    return pl.pallas_call(
        _combine_ln_body,
        grid_spec=pltpu.PrefetchScalarGridSpec(
            num_scalar_prefetch=2,
            grid=(T_ALL // ROW_TILE,),
            in_specs=[pl.BlockSpec(memory_space=pl.ANY), row(D_MODEL), row(1), row(1), vec, vec],
            out_specs=row(D_MODEL),
            scratch_shapes=[pltpu.VMEM((ROW_TILE, D_MODEL), F32)] * 2 + [pltpu.SemaphoreType.DMA],
        ),
        out_shape=jax.ShapeDtypeStruct((T_ALL, D_MODEL), F32),
        compiler_params=_cparams("arbitrary"),
        name="moe_combine_ln2",
    )(dest0, dest1, ys, x, w0.reshape(T_ALL, 1), w1.reshape(T_ALL, 1), ln_g, ln_b)


def _moe_ln(x, router_params, w_gate, w_up, w_down, ln_g, ln_b, layer):
    info, counts = _router(x, *router_params)
    dest0, dest1, tile, tile_expert, n_valid, last_tile = _routing_tables(info, counts)
    xs = _dispatch(x, dest0, dest1, last_tile, n_valid)
    ys = _experts(xs, tile, tile_expert, n_valid, w_gate, w_up, w_down, layer)
    return _combine_ln(ys, x, dest0, dest1, info[2], info[3], ln_g, ln_b, layer)


def kernel(x_prompt, x_sample, cache_k, cache_v, state_ssm_re, state_ssm_im, w_in, w_out, g_attn_out, g_ssm_out, ssm_a_re, ssm_a_im, ssm_log_dt, ssm_b_re, ssm_b_im, ssm_c_re, ssm_c_im, ssm_d, w_glu, ln1_g, ln1_b, ln2_g, ln2_b, w_router, b_router, w_gate, w_up, w_down):
    x = jnp.concatenate([x_prompt.reshape(T_PROMPT, D_MODEL), x_sample.reshape(T_SAMPLE, D_MODEL)], axis=0)
    x = x.astype(F32)
    w_in, w_out, w_glu = w_in.astype(BF16), w_out.astype(BF16), w_glu.astype(BF16)
    w_gate, w_up, w_down = w_gate.astype(BF16), w_up.astype(BF16), w_down.astype(BF16)
    vecs = [v.astype(F32).reshape(DEPTH, 1, -1) for v in (g_attn_out, g_ssm_out, ssm_d, ln1_g, ln1_b, ln2_g, ln2_b)]
    g_attn_out, g_ssm_out, ssm_d, ln1_g, ln1_b, ln2_g, ln2_b = vecs
    router_params = (w_router.astype(F32).T, b_router.astype(F32).reshape(N_EXPERTS, 1),
                     (jnp.arange(ROUTER_TILE)[:, None] < jnp.arange(ROUTER_TILE)[None, :]).astype(BF16))
    to_rows_minor = lambda c: c.transpose(0, 1, 3, 4, 2).reshape(DEPTH, DEC_BATCH, ATTN_WIDTH, PAST_LEN)
    cache_kt, cache_vt = to_rows_minor(cache_k), to_rows_minor(cache_v)
    ssm_tables = _ssm_tables(ssm_a_re, ssm_a_im, ssm_log_dt, ssm_b_re, ssm_b_im, ssm_c_re, ssm_c_im)

    kp_out, vp_out, srp_out, sip_out = [], [], [], []
    ks_out, vs_out, srs_out, sis_out = [], [], [], []
    for l in range(DEPTH):
        z = _matmul(x, w_in, l, tm=PROJECT_ROW_TILE, tn=ATTN_WIDTH, name="project_in")
        k_cols, v_cols = z[:, ATTN_WIDTH:2 * ATTN_WIDTH], z[:, 2 * ATTN_WIDTH:3 * ATTN_WIDTH]
        u = z[:, 3 * ATTN_WIDTH:]

        attn_p = _prompt_attention(z)
        attn_s = _sample_attention(z, cache_kt, cache_vt, l)

        y_p, h_p = _ssm_prompt(u[:T_PROMPT], _ssm_matrices(ssm_tables, l, SSM_CHUNK))
        y_s, h_s = _ssm_sample(u[T_PROMPT:], _state_to_pairs(state_ssm_re[l], state_ssm_im[l]),
                               _ssm_matrices(ssm_tables, l, DEC_SEQ))
        ssm = _glu(jnp.concatenate([y_p, y_s], axis=0), z, ssm_d, w_glu, l)

        x = _merge(attn_p, attn_s, ssm, x, g_attn_out, g_ssm_out, w_out, ln1_g, ln1_b, l)
        x = _moe_ln(x, router_params, w_gate, w_up, w_down, ln2_g, ln2_b, l)

        heads = lambda a, b: a.reshape(b, -1, N_HEADS, HEAD_DIM)
        window = min(2048, SEQ)
        kp_out.append(heads(k_cols[T_PROMPT - window:T_PROMPT], 1))
        vp_out.append(heads(v_cols[T_PROMPT - window:T_PROMPT], 1))
        ks_out.append(heads(k_cols[T_PROMPT:], DEC_BATCH))
        vs_out.append(heads(v_cols[T_PROMPT:], DEC_BATCH))
        re_p, im_p = _pairs_to_state(h_p[:, None, :])
        re_s, im_s = _pairs_to_state(h_s)
        srp_out.append(re_p)
        sip_out.append(im_p)
        srs_out.append(re_s)
        sis_out.append(im_s)
    return (x[:T_PROMPT].reshape(1, SEQ, D_MODEL), x[T_PROMPT:].reshape(DEC_BATCH, DEC_SEQ, D_MODEL),
            jnp.stack(kp_out), jnp.stack(vp_out), jnp.stack(srp_out), jnp.stack(sip_out),
            jnp.stack(ks_out), jnp.stack(vs_out), jnp.stack(srs_out), jnp.stack(sis_out))
```

```python
import functools
import math

import jax
import jax.numpy as jnp
from jax import lax
from jax.experimental import pallas as pl
from jax.experimental.pallas import tpu as pltpu

F32 = jnp.float32
BF16 = jnp.bfloat16
I32 = jnp.int32

D_MODEL = 2048
SEQ = 8192
DEPTH = 2
DEC_BATCH = 128
DEC_SEQ = 4
PAST_LEN = 2048
ATTN_WIDTH = D_MODEL // 2
SSM_WIDTH = D_MODEL - ATTN_WIDTH
HEAD_DIM = 64
N_HEADS = ATTN_WIDTH // HEAD_DIM
DILATED_BRANCHES = ((128, 1), (512, 4), (2048, 16))
LOOKBACK = 128
SSM_GROUP = 16
N_SSM_GROUPS = SSM_WIDTH // SSM_GROUP
N_PAIRS = N_SSM_GROUPS // 2
SSM_STATE = 64
PAIR_STATE = 4 * SSM_STATE
IN_COLS = 3 * ATTN_WIDTH + SSM_WIDTH
N_EXPERTS = 16
N_EXPERT_GROUPS = 4
EXPERTS_PER_GROUP = N_EXPERTS // N_EXPERT_GROUPS
D_EXPERT = D_MODEL // 2
DEEPNORM_ALPHA = (2.0 * DEPTH) ** 0.25
NORM_EPS = 1e-5

T_PROMPT = SEQ
T_SAMPLE = DEC_BATCH * DEC_SEQ
T_ALL = T_PROMPT + T_SAMPLE
SSM_CHUNK = 16
N_CHUNKS = T_PROMPT // SSM_CHUNK

V7X_VMEM_BYTES = 64 * 1024 * 1024
VMEM_LIMIT = V7X_VMEM_BYTES - 8 * 1024 * 1024

ROW_TILE = 256
MATMUL_ROW_TILE = 512
PROJECT_ROW_TILE = T_ALL // 8
EXPERT_TILE = 512
N_SLOTS = 2 * T_ALL + N_EXPERTS * EXPERT_TILE
N_EXPERT_TILES = N_SLOTS // EXPERT_TILE

ATTN_SCALE = 1.0 / math.sqrt(HEAD_DIM)
assert math.frexp(ATTN_SCALE)[0] == 0.5, "queries are pre-scaled before rounding, which needs a power of two"
SLOPE_LOG2_STEP = -8.0 / N_HEADS


def _cparams(*semantics):
    return pltpu.CompilerParams(dimension_semantics=semantics, vmem_limit_bytes=VMEM_LIMIT)


def _mm_body(x_ref, w_ref, o_ref):
    o_ref[...] = jnp.dot(x_ref[...].astype(BF16), w_ref[...], preferred_element_type=F32)


def _matmul(x, w, layer, *, tm, tn, name):
    m, k = x.shape
    n = w.shape[-1]
    return pl.pallas_call(
        _mm_body,
        grid=(n // tn, m // tm),
        in_specs=[pl.BlockSpec((tm, k), lambda j, i: (i, 0)),
                  pl.BlockSpec((None, k, tn), lambda j, i: (layer, 0, j))],
        out_specs=pl.BlockSpec((tm, tn), lambda j, i: (i, j)),
        out_shape=jax.ShapeDtypeStruct((m, n), F32),
        compiler_params=_cparams("parallel", "parallel"),
        name=name,
    )(x, w)


SUPER_BLOCK = 2048
PAIR_WIDTH = 2 * HEAD_DIM
BLOCK_UNROLL = 4


def _prompt_attn_body(q_ref, kp_ref, kc_ref, vp_ref, vc_ref, o_ref, k_all, v_all, o_br, lse_br):
    sb = pl.program_id(0)
    hp = pl.program_id(1)
    blk = LOOKBACK
    k_all[0:SUPER_BLOCK, :] = kp_ref[...]
    k_all[SUPER_BLOCK:2 * SUPER_BLOCK, :] = kc_ref[...]
    v_all[0:SUPER_BLOCK, :] = vp_ref[...]
    v_all[SUPER_BLOCK:2 * SUPER_BLOCK, :] = vc_ref[...]
    qi = lax.broadcasted_iota(I32, (blk, 2 * blk), 0)
    kj = lax.broadcasted_iota(I32, (blk, 2 * blk), 1)
    dist = qi + blk - kj
    in_band = (dist >= 0) & (dist <= LOOKBACK)
    low_lanes = lax.broadcasted_iota(I32, (blk, PAIR_WIDTH), 1) < HEAD_DIM
    head_lanes = (low_lanes, jnp.logical_not(low_lanes))
    slopes = [jnp.exp2(jnp.full((blk, 1), SLOPE_LOG2_STEP, F32) * (2 * hp + sub + 1).astype(F32))
              for sub in range(2)]
    nt = (((1,), (1,)), ((), ()))

    for bi, (_, dil) in enumerate(DILATED_BRANCHES):
        per_residue = SUPER_BLOCK // (blk * dil)
        bias = [slopes[sub] * (dist * dil).astype(F32) for sub in range(2)]
        stride = dil if dil > 1 else None

        def block(idx, carry, bi=bi, dil=dil, per_residue=per_residue, bias=bias, stride=stride):
            g = jnp.bitwise_and(idx, per_residue - 1)
            r = jnp.right_shift(idx, per_residue.bit_length() - 1)
            q0 = dil * blk * g + r
            kv0 = SUPER_BLOCK - dil * blk + q0
            if dil == 1:
                q0, kv0 = pl.multiple_of(q0, blk), pl.multiple_of(kv0, blk)
            q_rows = pl.ds(q0, blk, stride=stride)
            kv_rows = pl.ds(kv0, 2 * blk, stride=stride)
            first_key = jnp.where((sb == 0) & (g == 0), blk, 0)
            valid = in_band & (kj >= first_key)
            q2 = (q_ref[q_rows, :] * ATTN_SCALE).astype(BF16)
            k2 = k_all[kv_rows, :].astype(BF16)
            v2 = v_all[kv_rows, :].astype(BF16)
            outs, lses = [], []
            for sub in range(2):
                qm = jnp.where(head_lanes[sub], q2, jnp.zeros_like(q2))
                s = lax.dot_general(qm, k2, nt, preferred_element_type=F32)
                s = jnp.where(valid, s - bias[sub], -jnp.inf)
                m = jnp.max(s, axis=-1, keepdims=True)
                p = jnp.exp(s - m)
                l = jnp.sum(p, axis=-1, keepdims=True)
                o = jnp.dot(p.astype(BF16), v2, preferred_element_type=F32)
                outs.append(o / l)
                lses.append(jnp.broadcast_to(m + jnp.log(l), (blk, PAIR_WIDTH)))
            o_br[bi, q_rows, :] = jnp.where(low_lanes, outs[0], outs[1])
            lse_br[bi, q_rows, :] = jnp.where(low_lanes, lses[0], lses[1])
            return carry

        lax.fori_loop(0, dil * per_residue, block, 0, unroll=BLOCK_UNROLL)

    a, b, c = lse_br[0], lse_br[1], lse_br[2]
    mx = jnp.maximum(jnp.maximum(a, b), c)
    wa, wb, wc = jnp.exp(a - mx), jnp.exp(b - mx), jnp.exp(c - mx)
    o_ref[...] = (wa * o_br[0] + wb * o_br[1] + wc * o_br[2]) / (wa + wb + wc)


def _prompt_attention(z):
    n_pairs = ATTN_WIDTH // PAIR_WIDTH
    blk = (SUPER_BLOCK, PAIR_WIDTH)
    prev = lambda i: jnp.maximum(i - 1, 0)
    return pl.pallas_call(
        _prompt_attn_body,
        grid=(T_PROMPT // SUPER_BLOCK, n_pairs),
        in_specs=[pl.BlockSpec(blk, lambda i, h: (i, h)),
                  pl.BlockSpec(blk, lambda i, h: (prev(i), n_pairs + h)),
                  pl.BlockSpec(blk, lambda i, h: (i, n_pairs + h)),
                  pl.BlockSpec(blk, lambda i, h: (prev(i), 2 * n_pairs + h)),
                  pl.BlockSpec(blk, lambda i, h: (i, 2 * n_pairs + h))],
        out_specs=pl.BlockSpec(blk, lambda i, h: (i, h)),
        out_shape=jax.ShapeDtypeStruct((T_PROMPT, ATTN_WIDTH), F32),
        scratch_shapes=[pltpu.VMEM((2 * SUPER_BLOCK, PAIR_WIDTH), F32)] * 2
                       + [pltpu.VMEM((len(DILATED_BRANCHES), SUPER_BLOCK, PAIR_WIDTH), F32)] * 2,
        compiler_params=_cparams("parallel", "parallel"),
        name="prompt_attention",
    )(z, z, z, z, z)


NEW_ROWS_PAD = 16
OUT_ROWS_PAD = 8


def _sample_attn_body(zs_ref, kt_ref, vt_ref, o_ref, k_scr, v_scr):
    s_len = DEC_SEQ
    rows = s_len * N_HEADS
    head_shift = N_HEADS.bit_length() - 1
    zs = zs_ref[...]
    q = zs[:, 0:ATTN_WIDTH]
    k_new = zs[:, ATTN_WIDTH:2 * ATTN_WIDTH].astype(BF16)
    v_new = zs[:, 2 * ATTN_WIDTH:3 * ATTN_WIDTH].astype(BF16)
    k_scr[...] = kt_ref[...].astype(BF16)
    v_scr[...] = vt_ref[...].astype(BF16)

    lane_head = jnp.right_shift(lax.broadcasted_iota(I32, (N_HEADS, ATTN_WIDTH), 1), HEAD_DIM.bit_length() - 1)
    head_mask = lane_head == lax.broadcasted_iota(I32, (N_HEADS, ATTN_WIDTH), 0)
    qbd = jnp.concatenate(
        [jnp.where(head_mask, jnp.broadcast_to(q[s:s + 1, :], (N_HEADS, ATTN_WIDTH)), 0.0)
         for s in range(s_len)], axis=0).astype(BF16)
    row_head = jnp.bitwise_and(lax.broadcasted_iota(I32, (rows, 1), 0), N_HEADS - 1)
    slope = jnp.exp2((row_head + 1).astype(F32) * SLOPE_LOG2_STEP)
    nt = (((1,), (1,)), ((), ()))

    sc = jnp.dot(qbd, k_scr[...], preferred_element_type=F32)
    dist = (PAST_LEN + jnp.right_shift(lax.broadcasted_iota(I32, (rows, PAST_LEN), 0), head_shift)
            - lax.broadcasted_iota(I32, (rows, PAST_LEN), 1))
    mult = jnp.zeros((rows, PAST_LEN), F32)
    for window, dil in DILATED_BRANCHES:
        mult = mult + jnp.where((dist <= window) & (jnp.bitwise_and(dist, dil - 1) == 0), 1.0, 0.0)
    sc = jnp.where(mult > 0, sc * ATTN_SCALE - slope * dist.astype(F32), -jnp.inf)

    sn = lax.dot_general(qbd, k_new, nt, preferred_element_type=F32)
    col_n = lax.broadcasted_iota(I32, (rows, NEW_ROWS_PAD), 1)
    dist_n = jnp.right_shift(lax.broadcasted_iota(I32, (rows, NEW_ROWS_PAD), 0), head_shift) - col_n
    mult_n = jnp.where((dist_n >= 0) & (col_n < s_len),
                       jnp.where(dist_n == 0, float(len(DILATED_BRANCHES)), 1.0), 0.0)
    sn = jnp.where(mult_n > 0, sn * ATTN_SCALE - slope * dist_n.astype(F32), -jnp.inf)

    m = jnp.maximum(jnp.max(sc, axis=-1, keepdims=True), jnp.max(sn, axis=-1, keepdims=True))
    pc = mult * jnp.exp(sc - m)
    pn = mult_n * jnp.exp(sn - m)
    l = jnp.sum(pc, axis=-1, keepdims=True) + jnp.sum(pn, axis=-1, keepdims=True)
    acc = (lax.dot_general(pc.astype(BF16), v_scr[...], nt, preferred_element_type=F32)
           + jnp.dot(pn.astype(BF16), v_new, preferred_element_type=F32)) / l
    o_ref[...] = jnp.zeros_like(o_ref)
    for s in range(s_len):
        o_ref[s:s + 1, :] = jnp.sum(jnp.where(head_mask, acc[s * N_HEADS:(s + 1) * N_HEADS, :], 0.0),
                                    axis=0, keepdims=True)


def _sample_attention(z, cache_kt, cache_vt, layer):
    zs = jnp.pad(z[T_PROMPT:].reshape(DEC_BATCH, DEC_SEQ, IN_COLS), ((0, 0), (0, NEW_ROWS_PAD - DEC_SEQ), (0, 0)))
    cache_spec = pl.BlockSpec((None, None, ATTN_WIDTH, PAST_LEN), lambda b: (layer, b, 0, 0))
    out = pl.pallas_call(
        _sample_attn_body,
        grid=(DEC_BATCH,),
        in_specs=[pl.BlockSpec((None, NEW_ROWS_PAD, IN_COLS), lambda b: (b, 0, 0)), cache_spec, cache_spec],
        out_specs=pl.BlockSpec((None, OUT_ROWS_PAD, ATTN_WIDTH), lambda b: (b, 0, 0)),
        out_shape=jax.ShapeDtypeStruct((DEC_BATCH, OUT_ROWS_PAD, ATTN_WIDTH), F32),
        scratch_shapes=[pltpu.VMEM((ATTN_WIDTH, PAST_LEN), BF16)] * 2,
        compiler_params=_cparams("parallel"),
        name="sample_attention",
    )(zs, cache_kt, cache_vt)
    return out[:, :DEC_SEQ].reshape(T_SAMPLE, ATTN_WIDTH)


def _ssm_tables(a_re, a_im, log_dt, b_re, b_im, c_re, c_im):
    depth = a_re.shape[0]
    dt = jnp.exp(log_dt.astype(F32))[:, None, :, None]
    a_re, a_im = a_re.astype(F32)[:, None], a_im.astype(F32)[:, None]

    def powers(ks):
        e = ks.astype(F32)[None, :, None, None]
        mag = jnp.exp(e * dt * a_re)
        return mag * jnp.cos(e * dt * a_im), mag * jnp.sin(e * dt * a_im)

    pos_re, pos_im = powers(jnp.arange(SSM_CHUNK + 1))
    neg_re, neg_im = powers(-jnp.arange(SSM_CHUNK))
    den = (a_re * a_re + a_im * a_im)[:, 0]
    n_re, n_im = pos_re[:, 1] - 1.0, pos_im[:, 1]
    f_re = (n_re * a_re[:, 0] + n_im * a_im[:, 0]) / den
    f_im = (n_im * a_re[:, 0] - n_re * a_im[:, 0]) / den
    b_re, b_im = b_re.astype(F32), b_im.astype(F32)
    bb_re = (f_re[..., None] * b_re - f_im[..., None] * b_im).transpose(0, 1, 3, 2)
    bb_im = (f_re[..., None] * b_im + f_im[..., None] * b_re).transpose(0, 1, 3, 2)

    def lanes(re, im):
        x = jnp.stack([re, im], axis=1)
        return x.reshape(depth, 2, -1, N_PAIRS, 2 * SSM_STATE).transpose(0, 3, 1, 2, 4)

    def own_lanes(re, im):
        x = jnp.stack([re, im], axis=1).reshape(depth, 2, N_PAIRS, 2, SSM_GROUP, SSM_STATE)
        x = jnp.einsum('drjgcn,gh->djrgchn', x, jnp.eye(2, dtype=F32))
        return x.reshape(depth, N_PAIRS, 2, 2, SSM_GROUP, 2 * SSM_STATE)

    return (lanes(pos_re, pos_im), lanes(neg_re, neg_im), own_lanes(bb_re, bb_im),
            own_lanes(c_re.astype(F32), c_im.astype(F32)))


QUAD = 4
N_QUADS = N_PAIRS // QUAD
QUAD_LANES = QUAD * 2 * SSM_GROUP
QUAD_STATE = QUAD * PAIR_STATE


def _ssm_build_body(pos_ref, neg_ref, bb_ref, cc_ref, w_ref, p_ref, rt_ref, x_scr, y_scr, *, chunk):
    per_token = 2 * SSM_GROUP
    pair_rows = chunk * per_token
    cmul = lambda ar, ai, br, bi: (ar * br - ai * bi, ar * bi + ai * br)
    power = lambda ref, jj, k: (ref[jj, 0, k:k + 1, :], ref[jj, 1, k:k + 1, :])
    zeros = lambda n: [jnp.zeros((SSM_GROUP, PAIR_STATE), F32)] * n
    in_state_slot = lambda jj, piece: jnp.concatenate(zeros(jj) + [piece] + zeros(QUAD - 1 - jj), axis=1).astype(BF16)
    for jj in range(QUAD):
        for gg in range(2):
            b_r, b_i, c_r, c_i = bb_ref[jj, 0, gg], bb_ref[jj, 1, gg], cc_ref[jj, 0, gg], cc_ref[jj, 1, gg]
            for s in range(chunk):
                local = slice(s * per_token + gg * SSM_GROUP, s * per_token + (gg + 1) * SSM_GROUP)
                row0 = s * QUAD_LANES + jj * per_token + gg * SSM_GROUP
                placed = slice(row0, row0 + SSM_GROUP)
                p_r, p_i = cmul(*power(pos_ref, jj, chunk - 1 - s), b_r, b_i)
                p_ref[placed, :] = in_state_slot(jj, jnp.concatenate([p_r, p_i], axis=1))
                x_r, x_i = cmul(*power(neg_ref, jj, s), b_r, b_i)
                x_scr[local, :] = jnp.concatenate([x_r, x_i], axis=1)
                y_r, y_i = cmul(*power(pos_ref, jj, s), c_r, c_i)
                y_scr[local, :] = jnp.concatenate([y_r, -y_i], axis=1)
                r_r, r_i = cmul(*power(pos_ref, jj, s + 1), c_r, c_i)
                rt_ref[placed, :] = in_state_slot(jj, jnp.concatenate([r_r, -r_i], axis=1))
        w = lax.dot_general(x_scr[...], y_scr[...], _NT, precision=lax.Precision.HIGHEST,
                            preferred_element_type=F32)
        tok = lambda dim: jnp.right_shift(lax.broadcasted_iota(I32, (pair_rows, pair_rows), dim),
                                          per_token.bit_length() - 1)
        w = jnp.where(tok(1) >= tok(0), w, 0.0).astype(BF16)
        src = lax.broadcasted_iota(I32, (pair_rows, chunk * QUAD_LANES), 0)
        dst = lax.broadcasted_iota(I32, (pair_rows, chunk * QUAD_LANES), 1)
        same_token = jnp.right_shift(dst, QUAD_LANES.bit_length() - 1) == jnp.right_shift(src, per_token.bit_length() - 1)
        same_lane = jnp.bitwise_and(dst, QUAD_LANES - 1) == jj * per_token + jnp.bitwise_and(src, per_token - 1)
        place = jnp.where(same_token & same_lane, 1.0, 0.0).astype(BF16)
        w_cols = jnp.dot(w, place, preferred_element_type=F32).astype(BF16)
        for s in range(chunk):
            row0 = s * QUAD_LANES + jj * per_token
            w_ref[row0:row0 + per_token, :] = w_cols[s * per_token:(s + 1) * per_token, :]


def _ssm_matrices(tables, layer, chunk):
    pos, neg, bb, cc = tables
    tc = chunk * QUAD_LANES
    table = lambda t: pl.BlockSpec((None, QUAD) + t.shape[2:], lambda q: (layer, q) + (0,) * (t.ndim - 2))
    out = lambda width: pl.BlockSpec((None, tc, width), lambda q: (q, 0, 0))
    w, p, rt = pl.pallas_call(
        functools.partial(_ssm_build_body, chunk=chunk),
        grid=(N_QUADS,),
        in_specs=[table(pos), table(neg), table(bb), table(cc)],
        out_specs=[out(tc), out(QUAD_STATE), out(QUAD_STATE)],
        out_shape=[jax.ShapeDtypeStruct((N_QUADS, tc, tc), BF16),
                   jax.ShapeDtypeStruct((N_QUADS, tc, QUAD_STATE), BF16),
                   jax.ShapeDtypeStruct((N_QUADS, tc, QUAD_STATE), BF16)],
        scratch_shapes=[pltpu.VMEM((chunk * 2 * SSM_GROUP, PAIR_STATE), F32)] * 2,
        compiler_params=_cparams("parallel"),
        name="ssm_build",
    )(pos, neg, bb, cc)
    at_re, at_im = pos[layer, :, 0, chunk], pos[layer, :, 1, chunk]
    m1 = jnp.concatenate([at_re, at_re], axis=-1).reshape(1, N_PAIRS * PAIR_STATE)
    m2 = jnp.concatenate([-at_im, at_im], axis=-1).reshape(1, N_PAIRS * PAIR_STATE)
    return w, p, rt, m1, m2


def _swap_halves(h, width):
    parts = []
    for c in range(h.shape[-1] // (2 * width)):
        parts += [h[..., (2 * c + 1) * width:(2 * c + 2) * width], h[..., 2 * c * width:(2 * c + 1) * width]]
    return jnp.concatenate(parts, axis=-1)


def _chunk_inputs(u_ref, n_chunks, chunk):
    rows = [u_ref[pl.ds(t, n_chunks, stride=chunk), :] for t in range(chunk)]
    return jnp.concatenate(rows, axis=1).astype(BF16)


def _ssm_z_body(u_ref, p_ref, z_ref, *, n_chunks, chunk):
    z_ref[...] = jnp.dot(_chunk_inputs(u_ref, n_chunks, chunk), p_ref[...], preferred_element_type=F32)


def _ssm_scan_body(z_ref, m1_ref, m2_ref, hin_ref, hlast_ref, h_scr):
    @pl.when(pl.program_id(0) == 0)
    def _():
        h_scr[...] = jnp.zeros_like(h_scr)

    m1, m2 = m1_ref[...], m2_ref[...]

    def step(k, h):
        hin_ref[k] = h
        return m1 * h + m2 * _swap_halves(h, 2 * SSM_STATE) + z_ref[k]

    h = lax.fori_loop(0, z_ref.shape[0], step, h_scr[...])
    h_scr[...] = h
    hlast_ref[...] = h


_NT = (((1,), (1,)), ((), ()))


def _store_chunk_outputs(y_ref, y, n_chunks, chunk):
    for t in range(chunk):
        y_ref[pl.ds(t, n_chunks, stride=chunk), :] = y[:, t * QUAD_LANES:(t + 1) * QUAD_LANES]


def _ssm_y_body(u_ref, w_ref, hin_ref, rt_ref, y_ref, *, n_chunks, chunk):
    y = (jnp.dot(_chunk_inputs(u_ref, n_chunks, chunk), w_ref[...], preferred_element_type=F32)
         + lax.dot_general(hin_ref[...].astype(BF16), rt_ref[...], _NT, preferred_element_type=F32))
    _store_chunk_outputs(y_ref, y, n_chunks, chunk)


U_COL_BLOCK = 3 * ATTN_WIDTH // QUAD_LANES
_ONCE = pl.Buffered(1)


def _ssm_prompt(z_all, mats):
    w, p, rt, m1, m2 = mats
    tc = SSM_CHUNK * QUAD_LANES
    sub = 8
    state_w = N_PAIRS * PAIR_STATE // sub
    u_spec = pl.BlockSpec((T_PROMPT, QUAD_LANES), lambda q: (0, U_COL_BLOCK + q))
    dims = dict(n_chunks=N_CHUNKS, chunk=SSM_CHUNK)
    z = pl.pallas_call(
        functools.partial(_ssm_z_body, **dims),
        grid=(N_QUADS,),
        in_specs=[u_spec, pl.BlockSpec((None, tc, QUAD_STATE), lambda q: (q, 0, 0))],
        out_specs=pl.BlockSpec((N_CHUNKS, QUAD_STATE), lambda q: (0, q)),
        out_shape=jax.ShapeDtypeStruct((N_CHUNKS, N_PAIRS * PAIR_STATE), F32),
        compiler_params=_cparams("parallel"),
        name="ssm_chunk_state",
    )(z_all, p)
    scan_rows = 128
    hin, hlast = pl.pallas_call(
        _ssm_scan_body,
        grid=(N_CHUNKS // scan_rows,),
        in_specs=[pl.BlockSpec((scan_rows, sub, state_w), lambda i: (i, 0, 0)),
                  pl.BlockSpec((sub, state_w), lambda i: (0, 0)),
                  pl.BlockSpec((sub, state_w), lambda i: (0, 0))],
        out_specs=[pl.BlockSpec((scan_rows, sub, state_w), lambda i: (i, 0, 0)),
                   pl.BlockSpec((sub, state_w), lambda i: (0, 0))],
        out_shape=[jax.ShapeDtypeStruct((N_CHUNKS, sub, state_w), F32),
                   jax.ShapeDtypeStruct((sub, state_w), F32)],
        scratch_shapes=[pltpu.VMEM((sub, state_w), F32)],
        compiler_params=_cparams("arbitrary"),
        name="ssm_scan",
    )(z.reshape(N_CHUNKS, sub, state_w), m1.reshape(sub, state_w), m2.reshape(sub, state_w))
    y = pl.pallas_call(
        functools.partial(_ssm_y_body, **dims),
        grid=(N_QUADS,),
        in_specs=[u_spec,
                  pl.BlockSpec((None, tc, tc), lambda q: (q, 0, 0), pipeline_mode=_ONCE),
                  pl.BlockSpec((N_CHUNKS, QUAD_STATE), lambda q: (0, q)),
                  pl.BlockSpec((None, tc, QUAD_STATE), lambda q: (q, 0, 0), pipeline_mode=_ONCE)],
        out_specs=pl.BlockSpec((T_PROMPT, QUAD_LANES), lambda q: (0, q)),
        out_shape=jax.ShapeDtypeStruct((T_PROMPT, SSM_WIDTH), F32),
        compiler_params=_cparams("parallel"),
        name="ssm_chunk_output",
    )(z_all, w, hin.reshape(N_CHUNKS, N_PAIRS * PAIR_STATE), rt)
    return y, hlast.reshape(1, N_PAIRS * PAIR_STATE)


def _ssm_sample_body(u_ref, h0_ref, p_ref, w_ref, rt_ref, m1_ref, m2_ref, y_ref, hl_ref):
    u = _chunk_inputs(u_ref, DEC_BATCH, DEC_SEQ)
    h0 = h0_ref[...]
    y = (jnp.dot(u, w_ref[...], preferred_element_type=F32)
         + lax.dot_general(h0.astype(BF16), rt_ref[...], _NT, preferred_element_type=F32))
    _store_chunk_outputs(y_ref, y, DEC_BATCH, DEC_SEQ)
    hl_ref[...] = (m1_ref[...] * h0 + m2_ref[...] * _swap_halves(h0, 2 * SSM_STATE)
                   + jnp.dot(u, p_ref[...], preferred_element_type=F32))


def _ssm_sample(z_all, h0, mats):
    w, p, rt, m1, m2 = mats
    tc = DEC_SEQ * QUAD_LANES
    per_quad = lambda rows, cols: pl.BlockSpec((None, rows, cols), lambda q: (q, 0, 0))
    state = lambda rows: pl.BlockSpec((rows, QUAD_STATE), lambda q: (0, q))
    return pl.pallas_call(
        _ssm_sample_body,
        grid=(N_QUADS,),
        in_specs=[pl.BlockSpec((T_SAMPLE, QUAD_LANES), lambda q: (T_PROMPT // T_SAMPLE, U_COL_BLOCK + q)),
                  state(DEC_BATCH), per_quad(tc, QUAD_STATE), per_quad(tc, tc), per_quad(tc, QUAD_STATE),
                  state(1), state(1)],
        out_specs=[pl.BlockSpec((T_SAMPLE, QUAD_LANES), lambda q: (0, q)), state(DEC_BATCH)],
        out_shape=[jax.ShapeDtypeStruct((T_SAMPLE, SSM_WIDTH), F32),
                   jax.ShapeDtypeStruct((DEC_BATCH, N_PAIRS * PAIR_STATE), F32)],
        compiler_params=_cparams("parallel"),
        name="ssm_sample",
    )(z_all, h0, p, w, rt, m1, m2)


def _state_to_pairs(re, im):
    b = re.shape[0]
    f = lambda x: x.astype(F32).reshape(b, N_PAIRS, 2 * SSM_STATE)
    return jnp.concatenate([f(re), f(im)], axis=-1).reshape(b, N_PAIRS * PAIR_STATE)


def _pairs_to_state(h):
    b = h.shape[0]
    h = h.reshape(b, N_PAIRS, PAIR_STATE)
    f = lambda x: x.reshape(b, N_SSM_GROUPS, SSM_STATE)
    return f(h[..., :2 * SSM_STATE]), f(h[..., 2 * SSM_STATE:])


def _glu_body(y_ref, u_ref, d_ref, w_ref, o_ref):
    y = jax.nn.gelu(y_ref[...] + d_ref[...] * u_ref[...])
    zz = jnp.dot(y.astype(BF16), w_ref[...], preferred_element_type=F32)
    val, gate = zz[:, :SSM_WIDTH], zz[:, SSM_WIDTH:]
    o_ref[...] = val / (1.0 + jnp.exp(-gate))


def _glu(y, z, d_skip, w_glu, layer):
    tm = MATMUL_ROW_TILE
    return pl.pallas_call(
        _glu_body,
        grid=(T_ALL // tm,),
        in_specs=[pl.BlockSpec((tm, SSM_WIDTH), lambda i: (i, 0)),
                  pl.BlockSpec((tm, SSM_WIDTH), lambda i: (i, 3 * ATTN_WIDTH // SSM_WIDTH)),
                  pl.BlockSpec((None, 1, SSM_WIDTH), lambda i: (layer, 0, 0)),
                  pl.BlockSpec((None, SSM_WIDTH, 2 * SSM_WIDTH), lambda i: (layer, 0, 0))],
        out_specs=pl.BlockSpec((tm, SSM_WIDTH), lambda i: (i, 0)),
        out_shape=jax.ShapeDtypeStruct((T_ALL, SSM_WIDTH), F32),
        compiler_params=_cparams("parallel"),
        name="ssm_glu",
    )(y, z, d_skip, w_glu)


def _layer_norm(x, g, b):
    mu = jnp.mean(x, axis=-1, keepdims=True)
    xc = x - mu
    var = jnp.mean(xc * xc, axis=-1, keepdims=True)
    return xc * lax.rsqrt(var + NORM_EPS) * g + b


def _rms_norm(x, g):
    return x * lax.rsqrt(jnp.mean(x * x, axis=-1, keepdims=True) + NORM_EPS) * g


def _merge_body(ap_ref, as_ref, ssm_ref, x_ref, ga_ref, gs_ref, wa_ref, ws_ref, g_ref, b_ref, o_ref):
    is_prompt = pl.program_id(0) < T_PROMPT // MATMUL_ROW_TILE
    attn = jnp.where(is_prompt, ap_ref[...], as_ref[...])
    an = _rms_norm(attn, ga_ref[...]).astype(BF16)
    sn = _rms_norm(ssm_ref[...], gs_ref[...]).astype(BF16)
    mix = (jnp.dot(an, wa_ref[...], preferred_element_type=F32)
           + jnp.dot(sn, ws_ref[...], preferred_element_type=F32))
    o_ref[...] = _layer_norm(DEEPNORM_ALPHA * x_ref[...] + mix, g_ref[...], b_ref[...])


def _merge(attn_p, attn_s, ssm, x, g_attn, g_ssm, w_out, ln_g, ln_b, layer):
    tm = MATMUL_ROW_TILE
    n_p = T_PROMPT // tm
    row = lambda width: pl.BlockSpec((tm, width), lambda i: (i, 0))
    vec = lambda width: pl.BlockSpec((None, 1, width), lambda i: (layer, 0, 0))
    half = lambda which: pl.BlockSpec((None, ATTN_WIDTH, D_MODEL), lambda i: (layer, which, 0))
    return pl.pallas_call(
        _merge_body,
        grid=(T_ALL // tm,),
        in_specs=[pl.BlockSpec((tm, ATTN_WIDTH), lambda i: (jnp.minimum(i, n_p - 1), 0)),
                  pl.BlockSpec((tm, ATTN_WIDTH), lambda i: (jnp.maximum(i - n_p, 0), 0)),
                  row(SSM_WIDTH), row(D_MODEL), vec(ATTN_WIDTH), vec(SSM_WIDTH), half(0), half(1),
                  vec(D_MODEL), vec(D_MODEL)],
        out_specs=row(D_MODEL),
        out_shape=jax.ShapeDtypeStruct((T_ALL, D_MODEL), F32),
        compiler_params=_cparams("parallel"),
        name="merge_project_ln1",
    )(attn_p, attn_s, ssm, x, g_attn, g_ssm, w_out, w_out, ln_g, ln_b)


ROUTER_TILE = 512
INFO_ROWS = 8


def _top2_of4(a, b, c, d):
    m1, n1 = jnp.maximum(a, b), jnp.minimum(a, b)
    m2, n2 = jnp.maximum(c, d), jnp.minimum(c, d)
    return jnp.maximum(m1, m2) + jnp.maximum(jnp.minimum(m1, m2), jnp.maximum(n1, n2))


def _router_body(x_ref, wr_ref, br_ref, tri_ref, info_ref, cnt_ref, carry):
    @pl.when(pl.program_id(0) == 0)
    def _():
        carry[...] = jnp.zeros_like(carry)

    tm = x_ref.shape[0]
    logits = lax.dot_general(wr_ref[...], x_ref[...], (((1,), (1,)), ((), ())),
                             precision=lax.Precision.HIGHEST, preferred_element_type=F32)
    ex = jnp.exp(logits - jnp.max(logits, axis=0, keepdims=True))
    scores = ex / jnp.sum(ex, axis=0, keepdims=True)
    sel = scores + br_ref[...]
    row = lambda a, i: a[i:i + 1, :]
    epg = EXPERTS_PER_GROUP
    grp = [_top2_of4(*[row(sel, epg * g + i) for i in range(epg)]) for g in range(N_EXPERT_GROUPS)]
    best, best_v = jnp.zeros((1, tm), I32), grp[0]
    for g in range(1, N_EXPERT_GROUPS):
        upd = grp[g] > best_v
        best, best_v = jnp.where(upd, g, best), jnp.where(upd, grp[g], best_v)

    def in_group(a, i):
        out = row(a, i)
        for g in range(1, N_EXPERT_GROUPS):
            out = jnp.where(best == g, row(a, epg * g + i), out)
        return out

    v = [in_group(sel, i) for i in range(epg)]
    sc = [in_group(scores, i) for i in range(epg)]
    i0, v0, s0 = jnp.zeros((1, tm), I32), v[0], sc[0]
    for i in range(1, epg):
        upd = v[i] > v0
        i0, v0, s0 = jnp.where(upd, i, i0), jnp.where(upd, v[i], v0), jnp.where(upd, sc[i], s0)
    i1, v1, s1 = jnp.zeros((1, tm), I32), jnp.full((1, tm), -jnp.inf, F32), jnp.zeros((1, tm), F32)
    for i in range(epg):
        upd = (i0 != i) & (v[i] > v1)
        i1, v1, s1 = jnp.where(upd, i, i1), jnp.where(upd, v[i], v1), jnp.where(upd, sc[i], s1)
    e0, e1 = best * epg + i0, best * epg + i1
    wsum = s0 + s1

    eidx = lax.broadcasted_iota(I32, (N_EXPERTS, tm), 0)
    hit0, hit1 = eidx == e0, eidx == e1
    onehot = jnp.where(hit0 | hit1, 1.0, 0.0)
    before = jnp.dot(onehot.astype(BF16), tri_ref[...], preferred_element_type=F32) + carry[...]
    rank0 = jnp.sum(jnp.where(hit0, before, 0.0), axis=0, keepdims=True)
    rank1 = jnp.sum(jnp.where(hit1, before, 0.0), axis=0, keepdims=True)
    total = carry[...] + jnp.sum(onehot, axis=1, keepdims=True)
    carry[...] = total
    cnt_ref[...] = jnp.broadcast_to(total, cnt_ref.shape)
    info_ref[...] = jnp.zeros_like(info_ref)
    for i, val in enumerate((e0.astype(F32), e1.astype(F32), s0 / wsum, s1 / wsum, rank0, rank1)):
        info_ref[i:i + 1, :] = val


def _router(x, w_router_t, b_router, tri):
    tm = ROUTER_TILE
    return pl.pallas_call(
        _router_body,
        grid=(T_ALL // tm,),
        in_specs=[pl.BlockSpec((tm, D_MODEL), lambda i: (i, 0)),
                  pl.BlockSpec((N_EXPERTS, D_MODEL), lambda i: (0, 0)),
                  pl.BlockSpec((N_EXPERTS, 1), lambda i: (0, 0)),
                  pl.BlockSpec((tm, tm), lambda i: (0, 0))],
        out_specs=[pl.BlockSpec((INFO_ROWS, tm), lambda i: (0, i)),
                   pl.BlockSpec((N_EXPERTS, 128), lambda i: (0, 0))],
        out_shape=[jax.ShapeDtypeStruct((INFO_ROWS, T_ALL), F32),
                   jax.ShapeDtypeStruct((N_EXPERTS, 128), F32)],
        scratch_shapes=[pltpu.VMEM((N_EXPERTS, 1), F32)],
        compiler_params=_cparams("arbitrary"),
        name="router",
    )(x, w_router_t, b_router, tri)


def _routing_tables(info, counts):
    cnt = counts[:, 0].astype(I32)
    tiles = (cnt + EXPERT_TILE - 1) // EXPERT_TILE
    tiles_end = jnp.cumsum(tiles)
    first_slot = (tiles_end - tiles) * EXPERT_TILE
    e0, e1 = info[0].astype(I32), info[1].astype(I32)
    dest0 = first_slot[e0] + info[4].astype(I32)
    dest1 = first_slot[e1] + info[5].astype(I32)
    n_valid = tiles_end[-1]
    tile = jnp.minimum(jnp.arange(N_EXPERT_TILES, dtype=I32), n_valid - 1)
    tile_expert = jnp.sum((tiles_end[None, :] <= tile[:, None]).astype(I32), axis=1)
    tile_expert = jnp.minimum(tile_expert, N_EXPERTS - 1)
    last_tile = jnp.where(tiles > 0, tiles_end - 1, -1).astype(I32)
    return dest0, dest1, tile, tile_expert, n_valid.reshape(1).astype(I32), last_tile


ROW_COPY_UNROLL = 8


def _row_copy(src_ref, src_row, dst_ref, dst_row, sem):
    return pltpu.make_async_copy(src_ref.at[pl.ds(src_row, 1)], dst_ref.at[pl.ds(dst_row, 1)], sem)


def _dispatch_body(dest0, dest1, last_tile, n_valid, x_ref, xs_ref, zero_scr, sem):
    step = pl.program_id(0)
    tm = x_ref.shape[0]

    @pl.when(step == 0)
    def _():
        zero_scr[...] = jnp.zeros_like(zero_scr)
        fill = lambda t: pltpu.make_async_copy(zero_scr, xs_ref.at[pl.ds(t * EXPERT_TILE, EXPERT_TILE)], sem)
        padded = [(last_tile[e], last_tile[e] >= 0) for e in range(N_EXPERTS)]
        unused = [(N_EXPERT_TILES - 1 - j, N_EXPERT_TILES - 1 - j >= n_valid[0]) for j in range(N_EXPERTS)]
        for t, needed in padded + unused:
            @pl.when(needed)
            def _():
                fill(t).start()
        for t, needed in padded + unused:
            @pl.when(needed)
            def _():
                fill(t).wait()

    base = step * tm

    def start(r, c):
        _row_copy(x_ref, r, xs_ref, dest0[base + r], sem).start()
        _row_copy(x_ref, r, xs_ref, dest1[base + r], sem).start()
        return c

    def wait(r, c):
        _row_copy(x_ref, r, xs_ref, dest0[base + r], sem).wait()
        _row_copy(x_ref, r, xs_ref, dest1[base + r], sem).wait()
        return c

    lax.fori_loop(0, tm, start, 0, unroll=ROW_COPY_UNROLL)
    lax.fori_loop(0, tm, wait, 0, unroll=ROW_COPY_UNROLL)


def _dispatch(x, dest0, dest1, last_tile, n_valid):
    return pl.pallas_call(
        _dispatch_body,
        grid_spec=pltpu.PrefetchScalarGridSpec(
            num_scalar_prefetch=4,
            grid=(T_ALL // ROW_TILE,),
            in_specs=[pl.BlockSpec((ROW_TILE, D_MODEL), lambda i, *_: (i, 0))],
            out_specs=pl.BlockSpec(memory_space=pl.ANY),
            scratch_shapes=[pltpu.VMEM((EXPERT_TILE, D_MODEL), F32), pltpu.SemaphoreType.DMA],
        ),
        out_shape=jax.ShapeDtypeStruct((N_SLOTS, D_MODEL), F32),
        compiler_params=_cparams("arbitrary"),
        name="moe_dispatch",
    )(dest0, dest1, last_tile, n_valid, x)


def _expert_body(tile, tile_expert, n_valid, x_ref, wg_ref, wu_ref, wd_ref, y_ref):
    used = pl.program_id(0) < n_valid[0]

    @pl.when(used)
    def _():
        x = x_ref[...].astype(BF16)
        g = jnp.dot(x, wg_ref[...], preferred_element_type=F32)
        u = jnp.dot(x, wu_ref[...], preferred_element_type=F32)
        hid = (g / (1.0 + jnp.exp(-g))) * u
        y_ref[...] = jnp.dot(hid.astype(BF16), wd_ref[...], preferred_element_type=F32)

    @pl.when(jnp.logical_not(used))
    def _():
        y_ref[...] = jnp.zeros_like(y_ref)


def _experts(xs, tile, tile_expert, n_valid, w_gate, w_up, w_down, layer):
    rows = pl.BlockSpec((EXPERT_TILE, D_MODEL), lambda i, tile, te, nv: (tile[i], 0))
    out_rows = pl.BlockSpec((EXPERT_TILE, D_MODEL), lambda i, tile, te, nv: (i, 0))
    weight = lambda k, n: pl.BlockSpec((None, None, k, n), lambda i, tile, te, nv: (layer, te[i], 0, 0))
    return pl.pallas_call(
        _expert_body,
        grid_spec=pltpu.PrefetchScalarGridSpec(
            num_scalar_prefetch=3,
            grid=(N_EXPERT_TILES,),
            in_specs=[rows, weight(D_MODEL, D_EXPERT), weight(D_MODEL, D_EXPERT), weight(D_EXPERT, D_MODEL)],
            out_specs=out_rows,
        ),
        out_shape=jax.ShapeDtypeStruct((N_SLOTS, D_MODEL), F32),
        compiler_params=_cparams("arbitrary"),
        name="moe_experts",
    )(tile, tile_expert, n_valid, xs, w_gate, w_up, w_down)


def _combine_ln_body(dest0, dest1, ys_ref, x_ref, w0_ref, w1_ref, g_ref, b_ref, o_ref, buf0, buf1, sem):
    tm = x_ref.shape[0]
    base = pl.program_id(0) * tm

    def start(r, c):
        _row_copy(ys_ref, dest0[base + r], buf0, r, sem).start()
        _row_copy(ys_ref, dest1[base + r], buf1, r, sem).start()
        return c

    def wait(r, c):
        _row_copy(ys_ref, dest0[base + r], buf0, r, sem).wait()
        _row_copy(ys_ref, dest1[base + r], buf1, r, sem).wait()
        return c

    lax.fori_loop(0, tm, start, 0, unroll=ROW_COPY_UNROLL)
    lax.fori_loop(0, tm, wait, 0, unroll=ROW_COPY_UNROLL)
    y = w0_ref[...] * buf0[...] + w1_ref[...] * buf1[...]
    o_ref[...] = _layer_norm(DEEPNORM_ALPHA * x_ref[...] + y, g_ref[...], b_ref[...])


def _combine_ln(ys, x, dest0, dest1, w0, w1, ln_g, ln_b, layer):
    row = lambda width: pl.BlockSpec((ROW_TILE, width), lambda i, *_: (i, 0))
    vec = pl.BlockSpec((None, 1, D_MODEL), lambda i, *_: (layer, 0, 0))
    return pl.pallas_call(
        _combine_ln_body,
        grid_spec=pltpu.PrefetchScalarGridSpec(
            num_scalar_prefetch=2,
            grid=(T_ALL // ROW_TILE,),
            in_specs=[pl.BlockSpec(memory_space=pl.ANY), row(D_MODEL), row(1), row(1), vec, vec],
            out_specs=row(D_MODEL),
            scratch_shapes=[pltpu.VMEM((ROW_TILE, D_MODEL), F32)] * 2 + [pltpu.SemaphoreType.DMA],
        ),
        out_shape=jax.ShapeDtypeStruct((T_ALL, D_MODEL), F32),
        compiler_params=_cparams("arbitrary"),
        name="moe_combine_ln2",
    )(dest0, dest1, ys, x, w0.reshape(T_ALL, 1), w1.reshape(T_ALL, 1), ln_g, ln_b)


def _moe_ln(x, router_params, w_gate, w_up, w_down, ln_g, ln_b, layer):
    info, counts = _router(x, *router_params)
    dest0, dest1, tile, tile_expert, n_valid, last_tile = _routing_tables(info, counts)
    xs = _dispatch(x, dest0, dest1, last_tile, n_valid)
    ys = _experts(xs, tile, tile_expert, n_valid, w_gate, w_up, w_down, layer)
    return _combine_ln(ys, x, dest0, dest1, info[2], info[3], ln_g, ln_b, layer)


def kernel(x_prompt, x_sample, cache_k, cache_v, state_ssm_re, state_ssm_im, w_in, w_out, g_attn_out, g_ssm_out, ssm_a_re, ssm_a_im, ssm_log_dt, ssm_b_re, ssm_b_im, ssm_c_re, ssm_c_im, ssm_d, w_glu, ln1_g, ln1_b, ln2_g, ln2_b, w_router, b_router, w_gate, w_up, w_down):
    x = jnp.concatenate([x_prompt.reshape(T_PROMPT, D_MODEL), x_sample.reshape(T_SAMPLE, D_MODEL)], axis=0)
    x = x.astype(F32)
    w_in, w_out, w_glu = w_in.astype(BF16), w_out.astype(BF16), w_glu.astype(BF16)
    w_gate, w_up, w_down = w_gate.astype(BF16), w_up.astype(BF16), w_down.astype(BF16)
    vecs = [v.astype(F32).reshape(DEPTH, 1, -1) for v in (g_attn_out, g_ssm_out, ssm_d, ln1_g, ln1_b, ln2_g, ln2_b)]
    g_attn_out, g_ssm_out, ssm_d, ln1_g, ln1_b, ln2_g, ln2_b = vecs
    router_params = (w_router.astype(F32).T, b_router.astype(F32).reshape(N_EXPERTS, 1),
                     (jnp.arange(ROUTER_TILE)[:, None] < jnp.arange(ROUTER_TILE)[None, :]).astype(BF16))
    to_rows_minor = lambda c: c.transpose(0, 1, 3, 4, 2).reshape(DEPTH, DEC_BATCH, ATTN_WIDTH, PAST_LEN)
    cache_kt, cache_vt = to_rows_minor(cache_k), to_rows_minor(cache_v)
    ssm_tables = _ssm_tables(ssm_a_re, ssm_a_im, ssm_log_dt, ssm_b_re, ssm_b_im, ssm_c_re, ssm_c_im)

    kp_out, vp_out, srp_out, sip_out = [], [], [], []
    ks_out, vs_out, srs_out, sis_out = [], [], [], []
    for l in range(DEPTH):
        z = _matmul(x, w_in, l, tm=PROJECT_ROW_TILE, tn=ATTN_WIDTH, name="project_in")
        k_cols, v_cols = z[:, ATTN_WIDTH:2 * ATTN_WIDTH], z[:, 2 * ATTN_WIDTH:3 * ATTN_WIDTH]

        attn_p = _prompt_attention(z)
        attn_s = _sample_attention(z, cache_kt, cache_vt, l)

        y_p, h_p = _ssm_prompt(z, _ssm_matrices(ssm_tables, l, SSM_CHUNK))
        y_s, h_s = _ssm_sample(z, _state_to_pairs(state_ssm_re[l], state_ssm_im[l]),
                               _ssm_matrices(ssm_tables, l, DEC_SEQ))
        ssm = _glu(jnp.concatenate([y_p, y_s], axis=0), z, ssm_d, w_glu, l)

        x = _merge(attn_p, attn_s, ssm, x, g_attn_out, g_ssm_out, w_out, ln1_g, ln1_b, l)
        x = _moe_ln(x, router_params, w_gate, w_up, w_down, ln2_g, ln2_b, l)

        heads = lambda a, b: a.reshape(b, -1, N_HEADS, HEAD_DIM)
        window = min(2048, SEQ)
        kp_out.append(heads(k_cols[T_PROMPT - window:T_PROMPT], 1))
        vp_out.append(heads(v_cols[T_PROMPT - window:T_PROMPT], 1))
        ks_out.append(heads(k_cols[T_PROMPT:], DEC_BATCH))
        vs_out.append(heads(v_cols[T_PROMPT:], DEC_BATCH))
        re_p, im_p = _pairs_to_state(h_p)
        re_s, im_s = _pairs_to_state(h_s)
        srp_out.append(re_p)
        sip_out.append(im_p)
        srs_out.append(re_s)
        sis_out.append(im_s)
    return (x[:T_PROMPT].reshape(1, SEQ, D_MODEL), x[T_PROMPT:].reshape(DEC_BATCH, DEC_SEQ, D_MODEL),
            jnp.stack(kp_out), jnp.stack(vp_out), jnp.stack(srp_out), jnp.stack(sip_out),
            jnp.stack(ks_out), jnp.stack(vs_out), jnp.stack(srs_out), jnp.stack(sis_out))
```

```python
import functools
import math

import jax
import jax.numpy as jnp
from jax import lax
from jax.experimental import pallas as pl
from jax.experimental.pallas import tpu as pltpu

F32 = jnp.float32
BF16 = jnp.bfloat16
I32 = jnp.int32

D_MODEL = 2048
SEQ = 8192
DEPTH = 2
DEC_BATCH = 128
DEC_SEQ = 4
PAST_LEN = 2048
ATTN_WIDTH = D_MODEL // 2
SSM_WIDTH = D_MODEL - ATTN_WIDTH
HEAD_DIM = 64
N_HEADS = ATTN_WIDTH // HEAD_DIM
DILATED_BRANCHES = ((128, 1), (512, 4), (2048, 16))
LOOKBACK = 128
SSM_GROUP = 16
N_SSM_GROUPS = SSM_WIDTH // SSM_GROUP
N_PAIRS = N_SSM_GROUPS // 2
SSM_STATE = 64
PAIR_STATE = 4 * SSM_STATE
IN_COLS = 3 * ATTN_WIDTH + SSM_WIDTH
N_EXPERTS = 16
N_EXPERT_GROUPS = 4
EXPERTS_PER_GROUP = N_EXPERTS // N_EXPERT_GROUPS
D_EXPERT = D_MODEL // 2
DEEPNORM_ALPHA = (2.0 * DEPTH) ** 0.25
NORM_EPS = 1e-5

T_PROMPT = SEQ
T_SAMPLE = DEC_BATCH * DEC_SEQ
T_ALL = T_PROMPT + T_SAMPLE
SSM_CHUNK = 16
N_CHUNKS = T_PROMPT // SSM_CHUNK

V7X_VMEM_BYTES = 64 * 1024 * 1024
VMEM_LIMIT = V7X_VMEM_BYTES - 8 * 1024 * 1024

ROW_TILE = 256
MATMUL_ROW_TILE = 512
PROJECT_ROW_TILE = T_ALL // 8
EXPERT_TILE = 512
N_SLOTS = 2 * T_ALL + N_EXPERTS * EXPERT_TILE
N_EXPERT_TILES = N_SLOTS // EXPERT_TILE

ATTN_SCALE = 1.0 / math.sqrt(HEAD_DIM)
assert math.frexp(ATTN_SCALE)[0] == 0.5, "queries are pre-scaled before rounding, which needs a power of two"
SLOPE_LOG2_STEP = -8.0 / N_HEADS


def _cparams(*semantics):
    return pltpu.CompilerParams(dimension_semantics=semantics, vmem_limit_bytes=VMEM_LIMIT)


def _mm_body(x_ref, w_ref, o_ref):
    o_ref[...] = jnp.dot(x_ref[...].astype(BF16), w_ref[...], preferred_element_type=F32)


def _matmul(x, w, layer, *, tm, tn, name):
    m, k = x.shape
    n = w.shape[-1]
    return pl.pallas_call(
        _mm_body,
        grid=(n // tn, m // tm),
        in_specs=[pl.BlockSpec((tm, k), lambda j, i: (i, 0)),
                  pl.BlockSpec((None, k, tn), lambda j, i: (layer, 0, j))],
        out_specs=pl.BlockSpec((tm, tn), lambda j, i: (i, j)),
        out_shape=jax.ShapeDtypeStruct((m, n), F32),
        compiler_params=_cparams("parallel", "parallel"),
        name=name,
    )(x, w)


SUPER_BLOCK = 2048
PAIR_WIDTH = 2 * HEAD_DIM
BLOCK_UNROLL = 4


def _prompt_attn_body(q_ref, kp_ref, kc_ref, vp_ref, vc_ref, o_ref, k_all, v_all, o_br, lse_br):
    sb = pl.program_id(0)
    hp = pl.program_id(1)
    blk = LOOKBACK
    k_all[0:SUPER_BLOCK, :] = kp_ref[...]
    k_all[SUPER_BLOCK:2 * SUPER_BLOCK, :] = kc_ref[...]
    v_all[0:SUPER_BLOCK, :] = vp_ref[...]
    v_all[SUPER_BLOCK:2 * SUPER_BLOCK, :] = vc_ref[...]
    qi = lax.broadcasted_iota(I32, (blk, 2 * blk), 0)
    kj = lax.broadcasted_iota(I32, (blk, 2 * blk), 1)
    dist = qi + blk - kj
    in_band = (dist >= 0) & (dist <= LOOKBACK)
    low_lanes = lax.broadcasted_iota(I32, (blk, PAIR_WIDTH), 1) < HEAD_DIM
    head_lanes = (low_lanes, jnp.logical_not(low_lanes))
    slopes = [jnp.exp2(jnp.full((blk, 1), SLOPE_LOG2_STEP, F32) * (2 * hp + sub + 1).astype(F32))
              for sub in range(2)]
    nt = (((1,), (1,)), ((), ()))

    for bi, (_, dil) in enumerate(DILATED_BRANCHES):
        per_residue = SUPER_BLOCK // (blk * dil)
        bias = [slopes[sub] * (dist * dil).astype(F32) for sub in range(2)]
        stride = dil if dil > 1 else None

        def block(idx, carry, bi=bi, dil=dil, per_residue=per_residue, bias=bias, stride=stride):
            g = jnp.bitwise_and(idx, per_residue - 1)
            r = jnp.right_shift(idx, per_residue.bit_length() - 1)
            q0 = dil * blk * g + r
            kv0 = SUPER_BLOCK - dil * blk + q0
            if dil == 1:
                q0, kv0 = pl.multiple_of(q0, blk), pl.multiple_of(kv0, blk)
            q_rows = pl.ds(q0, blk, stride=stride)
            kv_rows = pl.ds(kv0, 2 * blk, stride=stride)
            first_key = jnp.where((sb == 0) & (g == 0), blk, 0)
            valid = in_band & (kj >= first_key)
            q2 = (q_ref[q_rows, :] * ATTN_SCALE).astype(BF16)
            k2 = k_all[kv_rows, :].astype(BF16)
            v2 = v_all[kv_rows, :].astype(BF16)
            outs, lses = [], []
            for sub in range(2):
                qm = jnp.where(head_lanes[sub], q2, jnp.zeros_like(q2))
                s = lax.dot_general(qm, k2, nt, preferred_element_type=F32)
                s = jnp.where(valid, s - bias[sub], -jnp.inf)
                m = jnp.max(s, axis=-1, keepdims=True)
                p = jnp.exp(s - m)
                l = jnp.sum(p, axis=-1, keepdims=True)
                o = jnp.dot(p.astype(BF16), v2, preferred_element_type=F32)
                outs.append(o / l)
                lses.append(jnp.broadcast_to(m + jnp.log(l), (blk, PAIR_WIDTH)))
            o_br[bi, q_rows, :] = jnp.where(low_lanes, outs[0], outs[1])
            lse_br[bi, q_rows, :] = jnp.where(low_lanes, lses[0], lses[1])
            return carry

        lax.fori_loop(0, dil * per_residue, block, 0, unroll=BLOCK_UNROLL)

    a, b, c = lse_br[0], lse_br[1], lse_br[2]
    mx = jnp.maximum(jnp.maximum(a, b), c)
    wa, wb, wc = jnp.exp(a - mx), jnp.exp(b - mx), jnp.exp(c - mx)
    o_ref[...] = (wa * o_br[0] + wb * o_br[1] + wc * o_br[2]) / (wa + wb + wc)


def _prompt_attention(z):
    n_pairs = ATTN_WIDTH // PAIR_WIDTH
    blk = (SUPER_BLOCK, PAIR_WIDTH)
    prev = lambda i: jnp.maximum(i - 1, 0)
    return pl.pallas_call(
        _prompt_attn_body,
        grid=(T_PROMPT // SUPER_BLOCK, n_pairs),
        in_specs=[pl.BlockSpec(blk, lambda i, h: (i, h)),
                  pl.BlockSpec(blk, lambda i, h: (prev(i), n_pairs + h)),
                  pl.BlockSpec(blk, lambda i, h: (i, n_pairs + h)),
                  pl.BlockSpec(blk, lambda i, h: (prev(i), 2 * n_pairs + h)),
                  pl.BlockSpec(blk, lambda i, h: (i, 2 * n_pairs + h))],
        out_specs=pl.BlockSpec(blk, lambda i, h: (i, h)),
        out_shape=jax.ShapeDtypeStruct((T_PROMPT, ATTN_WIDTH), F32),
        scratch_shapes=[pltpu.VMEM((2 * SUPER_BLOCK, PAIR_WIDTH), F32)] * 2
                       + [pltpu.VMEM((len(DILATED_BRANCHES), SUPER_BLOCK, PAIR_WIDTH), F32)] * 2,
        compiler_params=_cparams("parallel", "parallel"),
        name="prompt_attention",
    )(z, z, z, z, z)


NEW_ROWS_PAD = 16
OUT_ROWS_PAD = 8


def _sample_attn_body(zs_ref, kt_ref, vt_ref, o_ref, k_scr, v_scr):
    s_len = DEC_SEQ
    rows = s_len * N_HEADS
    head_shift = N_HEADS.bit_length() - 1
    zs = zs_ref[...]
    q = zs[:, 0:ATTN_WIDTH]
    k_new = zs[:, ATTN_WIDTH:2 * ATTN_WIDTH].astype(BF16)
    v_new = zs[:, 2 * ATTN_WIDTH:3 * ATTN_WIDTH].astype(BF16)
    k_scr[...] = kt_ref[...].astype(BF16)
    v_scr[...] = vt_ref[...].astype(BF16)

    lane_head = jnp.right_shift(lax.broadcasted_iota(I32, (N_HEADS, ATTN_WIDTH), 1), HEAD_DIM.bit_length() - 1)
    head_mask = lane_head == lax.broadcasted_iota(I32, (N_HEADS, ATTN_WIDTH), 0)
    qbd = jnp.concatenate(
        [jnp.where(head_mask, jnp.broadcast_to(q[s:s + 1, :], (N_HEADS, ATTN_WIDTH)), 0.0)
         for s in range(s_len)], axis=0).astype(BF16)
    row_head = jnp.bitwise_and(lax.broadcasted_iota(I32, (rows, 1), 0), N_HEADS - 1)
    slope = jnp.exp2((row_head + 1).astype(F32) * SLOPE_LOG2_STEP)
    nt = (((1,), (1,)), ((), ()))

    sc = jnp.dot(qbd, k_scr[...], preferred_element_type=F32)
    dist = (PAST_LEN + jnp.right_shift(lax.broadcasted_iota(I32, (rows, PAST_LEN), 0), head_shift)
            - lax.broadcasted_iota(I32, (rows, PAST_LEN), 1))
    mult = jnp.zeros((rows, PAST_LEN), F32)
    for window, dil in DILATED_BRANCHES:
        mult = mult + jnp.where((dist <= window) & (jnp.bitwise_and(dist, dil - 1) == 0), 1.0, 0.0)
    sc = jnp.where(mult > 0, sc * ATTN_SCALE - slope * dist.astype(F32), -jnp.inf)

    sn = lax.dot_general(qbd, k_new, nt, preferred_element_type=F32)
    col_n = lax.broadcasted_iota(I32, (rows, NEW_ROWS_PAD), 1)
    dist_n = jnp.right_shift(lax.broadcasted_iota(I32, (rows, NEW_ROWS_PAD), 0), head_shift) - col_n
    mult_n = jnp.where((dist_n >= 0) & (col_n < s_len),
                       jnp.where(dist_n == 0, float(len(DILATED_BRANCHES)), 1.0), 0.0)
    sn = jnp.where(mult_n > 0, sn * ATTN_SCALE - slope * dist_n.astype(F32), -jnp.inf)

    m = jnp.maximum(jnp.max(sc, axis=-1, keepdims=True), jnp.max(sn, axis=-1, keepdims=True))
    pc = mult * jnp.exp(sc - m)
    pn = mult_n * jnp.exp(sn - m)
    l = jnp.sum(pc, axis=-1, keepdims=True) + jnp.sum(pn, axis=-1, keepdims=True)
    acc = (lax.dot_general(pc.astype(BF16), v_scr[...], nt, preferred_element_type=F32)
           + jnp.dot(pn.astype(BF16), v_new, preferred_element_type=F32)) / l
    o_ref[...] = jnp.zeros_like(o_ref)
    for s in range(s_len):
        o_ref[s:s + 1, :] = jnp.sum(jnp.where(head_mask, acc[s * N_HEADS:(s + 1) * N_HEADS, :], 0.0),
                                    axis=0, keepdims=True)


def _sample_attention(z, cache_kt, cache_vt, layer):
    zs = jnp.pad(z[T_PROMPT:].reshape(DEC_BATCH, DEC_SEQ, IN_COLS), ((0, 0), (0, NEW_ROWS_PAD - DEC_SEQ), (0, 0)))
    cache_spec = pl.BlockSpec((None, None, ATTN_WIDTH, PAST_LEN), lambda b: (layer, b, 0, 0))
    out = pl.pallas_call(
        _sample_attn_body,
        grid=(DEC_BATCH,),
        in_specs=[pl.BlockSpec((None, NEW_ROWS_PAD, IN_COLS), lambda b: (b, 0, 0)), cache_spec, cache_spec],
        out_specs=pl.BlockSpec((None, OUT_ROWS_PAD, ATTN_WIDTH), lambda b: (b, 0, 0)),
        out_shape=jax.ShapeDtypeStruct((DEC_BATCH, OUT_ROWS_PAD, ATTN_WIDTH), F32),
        scratch_shapes=[pltpu.VMEM((ATTN_WIDTH, PAST_LEN), BF16)] * 2,
        compiler_params=_cparams("parallel"),
        name="sample_attention",
    )(zs, cache_kt, cache_vt)
    return out[:, :DEC_SEQ].reshape(T_SAMPLE, ATTN_WIDTH)


def _ssm_tables(a_re, a_im, log_dt, b_re, b_im, c_re, c_im):
    depth = a_re.shape[0]
    dt = jnp.exp(log_dt.astype(F32))[:, None, :, None]
    a_re, a_im = a_re.astype(F32)[:, None], a_im.astype(F32)[:, None]

    def powers(ks):
        e = ks.astype(F32)[None, :, None, None]
        mag = jnp.exp(e * dt * a_re)
        return mag * jnp.cos(e * dt * a_im), mag * jnp.sin(e * dt * a_im)

    pos_re, pos_im = powers(jnp.arange(SSM_CHUNK + 1))
    neg_re, neg_im = powers(-jnp.arange(SSM_CHUNK))
    den = (a_re * a_re + a_im * a_im)[:, 0]
    n_re, n_im = pos_re[:, 1] - 1.0, pos_im[:, 1]
    f_re = (n_re * a_re[:, 0] + n_im * a_im[:, 0]) / den
    f_im = (n_im * a_re[:, 0] - n_re * a_im[:, 0]) / den
    b_re, b_im = b_re.astype(F32), b_im.astype(F32)
    bb_re = (f_re[..., None] * b_re - f_im[..., None] * b_im).transpose(0, 1, 3, 2)
    bb_im = (f_re[..., None] * b_im + f_im[..., None] * b_re).transpose(0, 1, 3, 2)

    def lanes(re, im):
        x = jnp.stack([re, im], axis=1)
        return x.reshape(depth, 2, -1, N_PAIRS, 2 * SSM_STATE).transpose(0, 3, 1, 2, 4)

    def own_lanes(re, im):
        x = jnp.stack([re, im], axis=1).reshape(depth, 2, N_PAIRS, 2, SSM_GROUP, SSM_STATE)
        x = jnp.einsum('drjgcn,gh->djrgchn', x, jnp.eye(2, dtype=F32))
        return x.reshape(depth, N_PAIRS, 2, 2, SSM_GROUP, 2 * SSM_STATE)

    return (lanes(pos_re, pos_im), lanes(neg_re, neg_im), own_lanes(bb_re, bb_im),
            own_lanes(c_re.astype(F32), c_im.astype(F32)))


QUAD = 4
N_QUADS = N_PAIRS // QUAD
QUAD_LANES = QUAD * 2 * SSM_GROUP
QUAD_STATE = QUAD * PAIR_STATE


def _ssm_build_body(pos_ref, neg_ref, bb_ref, cc_ref, w_ref, p_ref, rt_ref, x_scr, y_scr, *, chunk):
    per_token = 2 * SSM_GROUP
    pair_rows = chunk * per_token
    cmul = lambda ar, ai, br, bi: (ar * br - ai * bi, ar * bi + ai * br)
    power = lambda ref, jj, k: (ref[jj, 0, k:k + 1, :], ref[jj, 1, k:k + 1, :])
    zeros = lambda n: [jnp.zeros((SSM_GROUP, PAIR_STATE), F32)] * n
    in_state_slot = lambda jj, piece: jnp.concatenate(zeros(jj) + [piece] + zeros(QUAD - 1 - jj), axis=1).astype(BF16)
    for jj in range(QUAD):
        for gg in range(2):
            b_r, b_i, c_r, c_i = bb_ref[jj, 0, gg], bb_ref[jj, 1, gg], cc_ref[jj, 0, gg], cc_ref[jj, 1, gg]
            for s in range(chunk):
                local = slice(s * per_token + gg * SSM_GROUP, s * per_token + (gg + 1) * SSM_GROUP)
                row0 = s * QUAD_LANES + jj * per_token + gg * SSM_GROUP
                placed = slice(row0, row0 + SSM_GROUP)
                p_r, p_i = cmul(*power(pos_ref, jj, chunk - 1 - s), b_r, b_i)
                p_ref[placed, :] = in_state_slot(jj, jnp.concatenate([p_r, p_i], axis=1))
                x_r, x_i = cmul(*power(neg_ref, jj, s), b_r, b_i)
                x_scr[local, :] = jnp.concatenate([x_r, x_i], axis=1)
                y_r, y_i = cmul(*power(pos_ref, jj, s), c_r, c_i)
                y_scr[local, :] = jnp.concatenate([y_r, -y_i], axis=1)
                r_r, r_i = cmul(*power(pos_ref, jj, s + 1), c_r, c_i)
                rt_ref[placed, :] = in_state_slot(jj, jnp.concatenate([r_r, -r_i], axis=1))
        w = lax.dot_general(x_scr[...], y_scr[...], _NT, precision=lax.Precision.HIGHEST,
                            preferred_element_type=F32)
        tok = lambda dim: jnp.right_shift(lax.broadcasted_iota(I32, (pair_rows, pair_rows), dim),
                                          per_token.bit_length() - 1)
        w = jnp.where(tok(1) >= tok(0), w, 0.0).astype(BF16)
        src = lax.broadcasted_iota(I32, (pair_rows, chunk * QUAD_LANES), 0)
        dst = lax.broadcasted_iota(I32, (pair_rows, chunk * QUAD_LANES), 1)
        same_token = jnp.right_shift(dst, QUAD_LANES.bit_length() - 1) == jnp.right_shift(src, per_token.bit_length() - 1)
        same_lane = jnp.bitwise_and(dst, QUAD_LANES - 1) == jj * per_token + jnp.bitwise_and(src, per_token - 1)
        place = jnp.where(same_token & same_lane, 1.0, 0.0).astype(BF16)
        w_cols = jnp.dot(w, place, preferred_element_type=F32).astype(BF16)
        for s in range(chunk):
            row0 = s * QUAD_LANES + jj * per_token
            w_ref[row0:row0 + per_token, :] = w_cols[s * per_token:(s + 1) * per_token, :]


def _ssm_matrices(tables, layer, chunk):
    pos, neg, bb, cc = tables
    tc = chunk * QUAD_LANES
    table = lambda t: pl.BlockSpec((None, QUAD) + t.shape[2:], lambda q: (layer, q) + (0,) * (t.ndim - 2))
    out = lambda width: pl.BlockSpec((None, tc, width), lambda q: (q, 0, 0))
    w, p, rt = pl.pallas_call(
        functools.partial(_ssm_build_body, chunk=chunk),
        grid=(N_QUADS,),
        in_specs=[table(pos), table(neg), table(bb), table(cc)],
        out_specs=[out(tc), out(QUAD_STATE), out(QUAD_STATE)],
        out_shape=[jax.ShapeDtypeStruct((N_QUADS, tc, tc), BF16),
                   jax.ShapeDtypeStruct((N_QUADS, tc, QUAD_STATE), BF16),
                   jax.ShapeDtypeStruct((N_QUADS, tc, QUAD_STATE), BF16)],
        scratch_shapes=[pltpu.VMEM((chunk * 2 * SSM_GROUP, PAIR_STATE), F32)] * 2,
        compiler_params=_cparams("parallel"),
        name="ssm_build",
    )(pos, neg, bb, cc)
    at_re, at_im = pos[layer, :, 0, chunk], pos[layer, :, 1, chunk]
    m1 = jnp.concatenate([at_re, at_re], axis=-1).reshape(1, N_PAIRS * PAIR_STATE)
    m2 = jnp.concatenate([-at_im, at_im], axis=-1).reshape(1, N_PAIRS * PAIR_STATE)
    return w, p, rt, m1, m2


def _swap_halves(h, width):
    parts = []
    for c in range(h.shape[-1] // (2 * width)):
        parts += [h[..., (2 * c + 1) * width:(2 * c + 2) * width], h[..., 2 * c * width:(2 * c + 1) * width]]
    return jnp.concatenate(parts, axis=-1)


def _chunk_inputs(u_ref, n_chunks, chunk):
    rows = [u_ref[pl.ds(t, n_chunks, stride=chunk), :] for t in range(chunk)]
    return jnp.concatenate(rows, axis=1).astype(BF16)


def _ssm_z_body(u_ref, p_ref, z_ref, *, n_chunks, chunk):
    z_ref[...] = jnp.dot(_chunk_inputs(u_ref, n_chunks, chunk), p_ref[...], preferred_element_type=F32)


def _ssm_scan_body(z_ref, m1_ref, m2_ref, hin_ref, hlast_ref, h_scr):
    @pl.when(pl.program_id(0) == 0)
    def _():
        h_scr[...] = jnp.zeros_like(h_scr)

    m1, m2 = m1_ref[...], m2_ref[...]

    def step(k, h):
        hin_ref[k] = h
        return m1 * h + m2 * _swap_halves(h, 2 * SSM_STATE) + z_ref[k]

    h = lax.fori_loop(0, z_ref.shape[0], step, h_scr[...])
    h_scr[...] = h
    hlast_ref[...] = h


_NT = (((1,), (1,)), ((), ()))


def _store_chunk_outputs(y_ref, y, n_chunks, chunk):
    for t in range(chunk):
        y_ref[pl.ds(t, n_chunks, stride=chunk), :] = y[:, t * QUAD_LANES:(t + 1) * QUAD_LANES]


def _ssm_y_body(u_ref, w_ref, hin_ref, rt_ref, y_ref, *, n_chunks, chunk):
    y = (jnp.dot(_chunk_inputs(u_ref, n_chunks, chunk), w_ref[...], preferred_element_type=F32)
         + lax.dot_general(hin_ref[...].astype(BF16), rt_ref[...], _NT, preferred_element_type=F32))
    _store_chunk_outputs(y_ref, y, n_chunks, chunk)


U_COL_BLOCK = 3 * ATTN_WIDTH // QUAD_LANES


def _ssm_prompt(z_all, mats):
    w, p, rt, m1, m2 = mats
    tc = SSM_CHUNK * QUAD_LANES
    sub = 8
    state_w = N_PAIRS * PAIR_STATE // sub
    u_spec = pl.BlockSpec((T_PROMPT, QUAD_LANES), lambda q: (0, U_COL_BLOCK + q))
    dims = dict(n_chunks=N_CHUNKS, chunk=SSM_CHUNK)
    z = pl.pallas_call(
        functools.partial(_ssm_z_body, **dims),
        grid=(N_QUADS,),
        in_specs=[u_spec, pl.BlockSpec((None, tc, QUAD_STATE), lambda q: (q, 0, 0))],
        out_specs=pl.BlockSpec((N_CHUNKS, QUAD_STATE), lambda q: (0, q)),
        out_shape=jax.ShapeDtypeStruct((N_CHUNKS, N_PAIRS * PAIR_STATE), F32),
        compiler_params=_cparams("parallel"),
        name="ssm_chunk_state",
    )(z_all, p)
    scan_rows = 128
    hin, hlast = pl.pallas_call(
        _ssm_scan_body,
        grid=(N_CHUNKS // scan_rows,),
        in_specs=[pl.BlockSpec((scan_rows, sub, state_w), lambda i: (i, 0, 0)),
                  pl.BlockSpec((sub, state_w), lambda i: (0, 0)),
                  pl.BlockSpec((sub, state_w), lambda i: (0, 0))],
        out_specs=[pl.BlockSpec((scan_rows, sub, state_w), lambda i: (i, 0, 0)),
                   pl.BlockSpec((sub, state_w), lambda i: (0, 0))],
        out_shape=[jax.ShapeDtypeStruct((N_CHUNKS, sub, state_w), F32),
                   jax.ShapeDtypeStruct((sub, state_w), F32)],
        scratch_shapes=[pltpu.VMEM((sub, state_w), F32)],
        compiler_params=_cparams("arbitrary"),
        name="ssm_scan",
    )(z.reshape(N_CHUNKS, sub, state_w), m1.reshape(sub, state_w), m2.reshape(sub, state_w))
    halves = 2
    y = pl.pallas_call(
        functools.partial(_ssm_y_body, n_chunks=N_CHUNKS // halves, chunk=SSM_CHUNK),
        grid=(N_QUADS, halves),
        in_specs=[pl.BlockSpec((T_PROMPT // halves, QUAD_LANES), lambda q, h: (h, U_COL_BLOCK + q)),
                  pl.BlockSpec((None, tc, tc), lambda q, h: (q, 0, 0)),
                  pl.BlockSpec((N_CHUNKS // halves, QUAD_STATE), lambda q, h: (h, q)),
                  pl.BlockSpec((None, tc, QUAD_STATE), lambda q, h: (q, 0, 0))],
        out_specs=pl.BlockSpec((T_PROMPT // halves, QUAD_LANES), lambda q, h: (h, q)),
        out_shape=jax.ShapeDtypeStruct((T_PROMPT, SSM_WIDTH), F32),
        compiler_params=_cparams("parallel", "parallel"),
        name="ssm_chunk_output",
    )(z_all, w, hin.reshape(N_CHUNKS, N_PAIRS * PAIR_STATE), rt)
    return y, hlast.reshape(1, N_PAIRS * PAIR_STATE)


def _ssm_sample_body(u_ref, h0_ref, p_ref, w_ref, rt_ref, m1_ref, m2_ref, y_ref, hl_ref):
    u = _chunk_inputs(u_ref, DEC_BATCH, DEC_SEQ)
    h0 = h0_ref[...]
    y = (jnp.dot(u, w_ref[...], preferred_element_type=F32)
         + lax.dot_general(h0.astype(BF16), rt_ref[...], _NT, preferred_element_type=F32))
    _store_chunk_outputs(y_ref, y, DEC_BATCH, DEC_SEQ)
    hl_ref[...] = (m1_ref[...] * h0 + m2_ref[...] * _swap_halves(h0, 2 * SSM_STATE)
                   + jnp.dot(u, p_ref[...], preferred_element_type=F32))


def _ssm_sample(z_all, h0, mats):
    w, p, rt, m1, m2 = mats
    tc = DEC_SEQ * QUAD_LANES
    per_quad = lambda rows, cols: pl.BlockSpec((None, rows, cols), lambda q: (q, 0, 0))
    state = lambda rows: pl.BlockSpec((rows, QUAD_STATE), lambda q: (0, q))
    return pl.pallas_call(
        _ssm_sample_body,
        grid=(N_QUADS,),
        in_specs=[pl.BlockSpec((T_SAMPLE, QUAD_LANES), lambda q: (T_PROMPT // T_SAMPLE, U_COL_BLOCK + q)),
                  state(DEC_BATCH), per_quad(tc, QUAD_STATE), per_quad(tc, tc), per_quad(tc, QUAD_STATE),
                  state(1), state(1)],
        out_specs=[pl.BlockSpec((T_SAMPLE, QUAD_LANES), lambda q: (0, q)), state(DEC_BATCH)],
        out_shape=[jax.ShapeDtypeStruct((T_SAMPLE, SSM_WIDTH), F32),
                   jax.ShapeDtypeStruct((DEC_BATCH, N_PAIRS * PAIR_STATE), F32)],
        compiler_params=_cparams("parallel"),
        name="ssm_sample",
    )(z_all, h0, p, w, rt, m1, m2)


def _state_to_pairs(re, im):
    b = re.shape[0]
    f = lambda x: x.astype(F32).reshape(b, N_PAIRS, 2 * SSM_STATE)
    return jnp.concatenate([f(re), f(im)], axis=-1).reshape(b, N_PAIRS * PAIR_STATE)


def _pairs_to_state(h):
    b = h.shape[0]
    h = h.reshape(b, N_PAIRS, PAIR_STATE)
    f = lambda x: x.reshape(b, N_SSM_GROUPS, SSM_STATE)
    return f(h[..., :2 * SSM_STATE]), f(h[..., 2 * SSM_STATE:])


def _glu_body(y_ref, u_ref, d_ref, w_ref, o_ref):
    y = jax.nn.gelu(y_ref[...] + d_ref[...] * u_ref[...])
    zz = jnp.dot(y.astype(BF16), w_ref[...], preferred_element_type=F32)
    val, gate = zz[:, :SSM_WIDTH], zz[:, SSM_WIDTH:]
    o_ref[...] = val / (1.0 + jnp.exp(-gate))


def _glu(y, z, d_skip, w_glu, layer):
    tm = MATMUL_ROW_TILE
    return pl.pallas_call(
        _glu_body,
        grid=(T_ALL // tm,),
        in_specs=[pl.BlockSpec((tm, SSM_WIDTH), lambda i: (i, 0)),
                  pl.BlockSpec((tm, SSM_WIDTH), lambda i: (i, 3 * ATTN_WIDTH // SSM_WIDTH)),
                  pl.BlockSpec((None, 1, SSM_WIDTH), lambda i: (layer, 0, 0)),
                  pl.BlockSpec((None, SSM_WIDTH, 2 * SSM_WIDTH), lambda i: (layer, 0, 0))],
        out_specs=pl.BlockSpec((tm, SSM_WIDTH), lambda i: (i, 0)),
        out_shape=jax.ShapeDtypeStruct((T_ALL, SSM_WIDTH), F32),
        compiler_params=_cparams("parallel"),
        name="ssm_glu",
    )(y, z, d_skip, w_glu)


def _layer_norm(x, g, b):
    mu = jnp.mean(x, axis=-1, keepdims=True)
    xc = x - mu
    var = jnp.mean(xc * xc, axis=-1, keepdims=True)
    return xc * lax.rsqrt(var + NORM_EPS) * g + b


def _rms_norm(x, g):
    return x * lax.rsqrt(jnp.mean(x * x, axis=-1, keepdims=True) + NORM_EPS) * g


def _merge_body(ap_ref, as_ref, ssm_ref, x_ref, ga_ref, gs_ref, wa_ref, ws_ref, g_ref, b_ref, o_ref):
    is_prompt = pl.program_id(0) < T_PROMPT // MATMUL_ROW_TILE
    attn = jnp.where(is_prompt, ap_ref[...], as_ref[...])
    an = _rms_norm(attn, ga_ref[...]).astype(BF16)
    sn = _rms_norm(ssm_ref[...], gs_ref[...]).astype(BF16)
    mix = (jnp.dot(an, wa_ref[...], preferred_element_type=F32)
           + jnp.dot(sn, ws_ref[...], preferred_element_type=F32))
    o_ref[...] = _layer_norm(DEEPNORM_ALPHA * x_ref[...] + mix, g_ref[...], b_ref[...])


def _merge(attn_p, attn_s, ssm, x, g_attn, g_ssm, w_out, ln_g, ln_b, layer):
    tm = MATMUL_ROW_TILE
    n_p = T_PROMPT // tm
    row = lambda width: pl.BlockSpec((tm, width), lambda i: (i, 0))
    vec = lambda width: pl.BlockSpec((None, 1, width), lambda i: (layer, 0, 0))
    half = lambda which: pl.BlockSpec((None, ATTN_WIDTH, D_MODEL), lambda i: (layer, which, 0))
    return pl.pallas_call(
        _merge_body,
        grid=(T_ALL // tm,),
        in_specs=[pl.BlockSpec((tm, ATTN_WIDTH), lambda i: (jnp.minimum(i, n_p - 1), 0)),
                  pl.BlockSpec((tm, ATTN_WIDTH), lambda i: (jnp.maximum(i - n_p, 0), 0)),
                  row(SSM_WIDTH), row(D_MODEL), vec(ATTN_WIDTH), vec(SSM_WIDTH), half(0), half(1),
                  vec(D_MODEL), vec(D_MODEL)],
        out_specs=row(D_MODEL),
        out_shape=jax.ShapeDtypeStruct((T_ALL, D_MODEL), F32),
        compiler_params=_cparams("parallel"),
        name="merge_project_ln1",
    )(attn_p, attn_s, ssm, x, g_attn, g_ssm, w_out, w_out, ln_g, ln_b)


ROUTER_TILE = 512
INFO_ROWS = 8


def _top2_of4(a, b, c, d):
    m1, n1 = jnp.maximum(a, b), jnp.minimum(a, b)
    m2, n2 = jnp.maximum(c, d), jnp.minimum(c, d)
    return jnp.maximum(m1, m2) + jnp.maximum(jnp.minimum(m1, m2), jnp.maximum(n1, n2))


def _router_body(x_ref, wr_ref, br_ref, tri_ref, info_ref, cnt_ref, carry):
    @pl.when(pl.program_id(0) == 0)
    def _():
        carry[...] = jnp.zeros_like(carry)

    tm = x_ref.shape[0]
    logits = lax.dot_general(wr_ref[...], x_ref[...], (((1,), (1,)), ((), ())),
                             precision=lax.Precision.HIGHEST, preferred_element_type=F32)
    ex = jnp.exp(logits - jnp.max(logits, axis=0, keepdims=True))
    scores = ex / jnp.sum(ex, axis=0, keepdims=True)
    sel = scores + br_ref[...]
    row = lambda a, i: a[i:i + 1, :]
    epg = EXPERTS_PER_GROUP
    grp = [_top2_of4(*[row(sel, epg * g + i) for i in range(epg)]) for g in range(N_EXPERT_GROUPS)]
    best, best_v = jnp.zeros((1, tm), I32), grp[0]
    for g in range(1, N_EXPERT_GROUPS):
        upd = grp[g] > best_v
        best, best_v = jnp.where(upd, g, best), jnp.where(upd, grp[g], best_v)

    def in_group(a, i):
        out = row(a, i)
        for g in range(1, N_EXPERT_GROUPS):
            out = jnp.where(best == g, row(a, epg * g + i), out)
        return out

    v = [in_group(sel, i) for i in range(epg)]
    sc = [in_group(scores, i) for i in range(epg)]
    i0, v0, s0 = jnp.zeros((1, tm), I32), v[0], sc[0]
    for i in range(1, epg):
        upd = v[i] > v0
        i0, v0, s0 = jnp.where(upd, i, i0), jnp.where(upd, v[i], v0), jnp.where(upd, sc[i], s0)
    i1, v1, s1 = jnp.zeros((1, tm), I32), jnp.full((1, tm), -jnp.inf, F32), jnp.zeros((1, tm), F32)
    for i in range(epg):
        upd = (i0 != i) & (v[i] > v1)
        i1, v1, s1 = jnp.where(upd, i, i1), jnp.where(upd, v[i], v1), jnp.where(upd, sc[i], s1)
    e0, e1 = best * epg + i0, best * epg + i1
    wsum = s0 + s1

    eidx = lax.broadcasted_iota(I32, (N_EXPERTS, tm), 0)
    hit0, hit1 = eidx == e0, eidx == e1
    onehot = jnp.where(hit0 | hit1, 1.0, 0.0)
    before = jnp.dot(onehot.astype(BF16), tri_ref[...], preferred_element_type=F32) + carry[...]
    rank0 = jnp.sum(jnp.where(hit0, before, 0.0), axis=0, keepdims=True)
    rank1 = jnp.sum(jnp.where(hit1, before, 0.0), axis=0, keepdims=True)
    total = carry[...] + jnp.sum(onehot, axis=1, keepdims=True)
    carry[...] = total
    cnt_ref[...] = jnp.broadcast_to(total, cnt_ref.shape)
    info_ref[...] = jnp.zeros_like(info_ref)
    for i, val in enumerate((e0.astype(F32), e1.astype(F32), s0 / wsum, s1 / wsum, rank0, rank1)):
        info_ref[i:i + 1, :] = val


def _router(x, w_router_t, b_router, tri):
    tm = ROUTER_TILE
    return pl.pallas_call(
        _router_body,
        grid=(T_ALL // tm,),
        in_specs=[pl.BlockSpec((tm, D_MODEL), lambda i: (i, 0)),
                  pl.BlockSpec((N_EXPERTS, D_MODEL), lambda i: (0, 0)),
                  pl.BlockSpec((N_EXPERTS, 1), lambda i: (0, 0)),
                  pl.BlockSpec((tm, tm), lambda i: (0, 0))],
        out_specs=[pl.BlockSpec((INFO_ROWS, tm), lambda i: (0, i)),
                   pl.BlockSpec((N_EXPERTS, 128), lambda i: (0, 0))],
        out_shape=[jax.ShapeDtypeStruct((INFO_ROWS, T_ALL), F32),
                   jax.ShapeDtypeStruct((N_EXPERTS, 128), F32)],
        scratch_shapes=[pltpu.VMEM((N_EXPERTS, 1), F32)],
        compiler_params=_cparams("arbitrary"),
        name="router",
    )(x, w_router_t, b_router, tri)


def _routing_tables(info, counts):
    cnt = counts[:, 0].astype(I32)
    tiles = (cnt + EXPERT_TILE - 1) // EXPERT_TILE
    tiles_end = jnp.cumsum(tiles)
    first_slot = (tiles_end - tiles) * EXPERT_TILE
    e0, e1 = info[0].astype(I32), info[1].astype(I32)
    dest0 = first_slot[e0] + info[4].astype(I32)
    dest1 = first_slot[e1] + info[5].astype(I32)
    n_valid = tiles_end[-1]
    tile = jnp.minimum(jnp.arange(N_EXPERT_TILES, dtype=I32), n_valid - 1)
    tile_expert = jnp.sum((tiles_end[None, :] <= tile[:, None]).astype(I32), axis=1)
    tile_expert = jnp.minimum(tile_expert, N_EXPERTS - 1)
    last_tile = jnp.where(tiles > 0, tiles_end - 1, -1).astype(I32)
    return dest0, dest1, tile, tile_expert, n_valid.reshape(1).astype(I32), last_tile


ROW_COPY_UNROLL = 8


def _row_copy(src_ref, src_row, dst_ref, dst_row, sem):
    return pltpu.make_async_copy(src_ref.at[pl.ds(src_row, 1)], dst_ref.at[pl.ds(dst_row, 1)], sem)


def _dispatch_body(dest0, dest1, last_tile, n_valid, x_ref, xs_ref, zero_scr, sem):
    step = pl.program_id(0)
    tm = x_ref.shape[0]

    @pl.when(step == 0)
    def _():
        zero_scr[...] = jnp.zeros_like(zero_scr)
        fill = lambda t: pltpu.make_async_copy(zero_scr, xs_ref.at[pl.ds(t * EXPERT_TILE, EXPERT_TILE)], sem)
        padded = [(last_tile[e], last_tile[e] >= 0) for e in range(N_EXPERTS)]
        unused = [(N_EXPERT_TILES - 1 - j, N_EXPERT_TILES - 1 - j >= n_valid[0]) for j in range(N_EXPERTS)]
        for t, needed in padded + unused:
            @pl.when(needed)
            def _():
                fill(t).start()
        for t, needed in padded + unused:
            @pl.when(needed)
            def _():
                fill(t).wait()

    base = step * tm

    def start(r, c):
        _row_copy(x_ref, r, xs_ref, dest0[base + r], sem).start()
        _row_copy(x_ref, r, xs_ref, dest1[base + r], sem).start()
        return c

    def wait(r, c):
        _row_copy(x_ref, r, xs_ref, dest0[base + r], sem).wait()
        _row_copy(x_ref, r, xs_ref, dest1[base + r], sem).wait()
        return c

    lax.fori_loop(0, tm, start, 0, unroll=ROW_COPY_UNROLL)
    lax.fori_loop(0, tm, wait, 0, unroll=ROW_COPY_UNROLL)


def _dispatch(x, dest0, dest1, last_tile, n_valid):
    return pl.pallas_call(
        _dispatch_body,
        grid_spec=pltpu.PrefetchScalarGridSpec(
            num_scalar_prefetch=4,
            grid=(T_ALL // ROW_TILE,),
            in_specs=[pl.BlockSpec((ROW_TILE, D_MODEL), lambda i, *_: (i, 0))],
            out_specs=pl.BlockSpec(memory_space=pl.ANY),
            scratch_shapes=[pltpu.VMEM((EXPERT_TILE, D_MODEL), F32), pltpu.SemaphoreType.DMA],
        ),
        out_shape=jax.ShapeDtypeStruct((N_SLOTS, D_MODEL), F32),
        compiler_params=_cparams("arbitrary"),
        name="moe_dispatch",
    )(dest0, dest1, last_tile, n_valid, x)


def _expert_body(tile, tile_expert, n_valid, x_ref, wg_ref, wu_ref, wd_ref, y_ref):
    used = pl.program_id(0) < n_valid[0]

    @pl.when(used)
    def _():
        x = x_ref[...].astype(BF16)
        g = jnp.dot(x, wg_ref[...], preferred_element_type=F32)
        u = jnp.dot(x, wu_ref[...], preferred_element_type=F32)
        hid = (g / (1.0 + jnp.exp(-g))) * u
        y_ref[...] = jnp.dot(hid.astype(BF16), wd_ref[...], preferred_element_type=F32)

    @pl.when(jnp.logical_not(used))
    def _():
        y_ref[...] = jnp.zeros_like(y_ref)


def _experts(xs, tile, tile_expert, n_valid, w_gate, w_up, w_down, layer):
    rows = pl.BlockSpec((EXPERT_TILE, D_MODEL), lambda i, tile, te, nv: (tile[i], 0))
    out_rows = pl.BlockSpec((EXPERT_TILE, D_MODEL), lambda i, tile, te, nv: (i, 0))
    weight = lambda k, n: pl.BlockSpec((None, None, k, n), lambda i, tile, te, nv: (layer, te[i], 0, 0))
    return pl.pallas_call(
        _expert_body,
        grid_spec=pltpu.PrefetchScalarGridSpec(
            num_scalar_prefetch=3,
            grid=(N_EXPERT_TILES,),
            in_specs=[rows, weight(D_MODEL, D_EXPERT), weight(D_MODEL, D_EXPERT), weight(D_EXPERT, D_MODEL)],
            out_specs=out_rows,
        ),
        out_shape=jax.ShapeDtypeStruct((N_SLOTS, D_MODEL), F32),
        compiler_params=_cparams("arbitrary"),
        name="moe_experts",
    )(tile, tile_expert, n_valid, xs, w_gate, w_up, w_down)


def _combine_ln_body(dest0, dest1, ys_ref, x_ref, w0_ref, w1_ref, g_ref, b_ref, o_ref, buf0, buf1, sem):
    tm = x_ref.shape[0]
    base = pl.program_id(0) * tm

    def start(r, c):
        _row_copy(ys_ref, dest0[base + r], buf0, r, sem).start()
        _row_copy(ys_ref, dest1[base + r], buf1, r, sem).start()
        return c

    def wait(r, c):
        _row_copy(ys_ref, dest0[base + r], buf0, r, sem).wait()
        _row_copy(ys_ref, dest1[base + r], buf1, r, sem).wait()
        return c

    lax.fori_loop(0, tm, start, 0, unroll=ROW_COPY_UNROLL)
    lax.fori_loop(0, tm, wait, 0, unroll=ROW_COPY_UNROLL)
    y = w0_ref[...] * buf0[...] + w1_ref[...] * buf1[...]
    o_ref[...] = _layer_norm(DEEPNORM_ALPHA * x_ref[...] + y, g_ref[...], b_ref[...])


def _combine_ln(ys, x, dest0, dest1, w0, w1, ln_g, ln_b, layer):
    row = lambda width: pl.BlockSpec((ROW_TILE, width), lambda i, *_: (i, 0))
    vec = pl.BlockSpec((None, 1, D_MODEL), lambda i, *_: (layer, 0, 0))
    return pl.pallas_call(
        _combine_ln_body,
        grid_spec=pltpu.PrefetchScalarGridSpec(
            num_scalar_prefetch=2,
            grid=(T_ALL // ROW_TILE,),
            in_specs=[pl.BlockSpec(memory_space=pl.ANY), row(D_MODEL), row(1), row(1), vec, vec],
            out_specs=row(D_MODEL),
            scratch_shapes=[pltpu.VMEM((ROW_TILE, D_MODEL), F32)] * 2 + [pltpu.SemaphoreType.DMA],
        ),
        out_shape=jax.ShapeDtypeStruct((T_ALL, D_MODEL), F32),
        compiler_params=_cparams("arbitrary"),
        name="moe_combine_ln2",
    )(dest0, dest1, ys, x, w0.reshape(T_ALL, 1), w1.reshape(T_ALL, 1), ln_g, ln_b)


def _moe_ln(x, router_params, w_gate, w_up, w_down, ln_g, ln_b, layer):
    info, counts = _router(x, *router_params)
    dest0, dest1, tile, tile_expert, n_valid, last_tile = _routing_tables(info, counts)
    xs = _dispatch(x, dest0, dest1, last_tile, n_valid)
    ys = _experts(xs, tile, tile_expert, n_valid, w_gate, w_up, w_down, layer)
    return _combine_ln(ys, x, dest0, dest1, info[2], info[3], ln_g, ln_b, layer)


def kernel(x_prompt, x_sample, cache_k, cache_v, state_ssm_re, state_ssm_im, w_in, w_out, g_attn_out, g_ssm_out, ssm_a_re, ssm_a_im, ssm_log_dt, ssm_b_re, ssm_b_im, ssm_c_re, ssm_c_im, ssm_d, w_glu, ln1_g, ln1_b, ln2_g, ln2_b, w_router, b_router, w_gate, w_up, w_down):
    x = jnp.concatenate([x_prompt.reshape(T_PROMPT, D_MODEL), x_sample.reshape(T_SAMPLE, D_MODEL)], axis=0)
    x = x.astype(F32)
    w_in, w_out, w_glu = w_in.astype(BF16), w_out.astype(BF16), w_glu.astype(BF16)
    w_gate, w_up, w_down = w_gate.astype(BF16), w_up.astype(BF16), w_down.astype(BF16)
    vecs = [v.astype(F32).reshape(DEPTH, 1, -1) for v in (g_attn_out, g_ssm_out, ssm_d, ln1_g, ln1_b, ln2_g, ln2_b)]
    g_attn_out, g_ssm_out, ssm_d, ln1_g, ln1_b, ln2_g, ln2_b = vecs
    router_params = (w_router.astype(F32).T, b_router.astype(F32).reshape(N_EXPERTS, 1),
                     (jnp.arange(ROUTER_TILE)[:, None] < jnp.arange(ROUTER_TILE)[None, :]).astype(BF16))
    to_rows_minor = lambda c: c.transpose(0, 1, 3, 4, 2).reshape(DEPTH, DEC_BATCH, ATTN_WIDTH, PAST_LEN)
    cache_kt, cache_vt = to_rows_minor(cache_k), to_rows_minor(cache_v)
    ssm_tables = _ssm_tables(ssm_a_re, ssm_a_im, ssm_log_dt, ssm_b_re, ssm_b_im, ssm_c_re, ssm_c_im)

    kp_out, vp_out, srp_out, sip_out = [], [], [], []
    ks_out, vs_out, srs_out, sis_out = [], [], [], []
    for l in range(DEPTH):
        z = _matmul(x, w_in, l, tm=PROJECT_ROW_TILE, tn=ATTN_WIDTH, name="project_in")
        k_cols, v_cols = z[:, ATTN_WIDTH:2 * ATTN_WIDTH], z[:, 2 * ATTN_WIDTH:3 * ATTN_WIDTH]

        attn_p = _prompt_attention(z)
        attn_s = _sample_attention(z, cache_kt, cache_vt, l)

        y_p, h_p = _ssm_prompt(z, _ssm_matrices(ssm_tables, l, SSM_CHUNK))
        y_s, h_s = _ssm_sample(z, _state_to_pairs(state_ssm_re[l], state_ssm_im[l]),
                               _ssm_matrices(ssm_tables, l, DEC_SEQ))
        ssm = _glu(jnp.concatenate([y_p, y_s], axis=0), z, ssm_d, w_glu, l)

        x = _merge(attn_p, attn_s, ssm, x, g_attn_out, g_ssm_out, w_out, ln1_g, ln1_b, l)
        x = _moe_ln(x, router_params, w_gate, w_up, w_down, ln2_g, ln2_b, l)

        heads = lambda a, b: a.reshape(b, -1, N_HEADS, HEAD_DIM)
        window = min(2048, SEQ)
        kp_out.append(heads(k_cols[T_PROMPT - window:T_PROMPT], 1))
        vp_out.append(heads(v_cols[T_PROMPT - window:T_PROMPT], 1))
        ks_out.append(heads(k_cols[T_PROMPT:], DEC_BATCH))
        vs_out.append(heads(v_cols[T_PROMPT:], DEC_BATCH))
        re_p, im_p = _pairs_to_state(h_p)
        re_s, im_s = _pairs_to_state(h_s)
        srp_out.append(re_p)
        sip_out.append(im_p)
        srs_out.append(re_s)
        sis_out.append(im_s)
    return (x[:T_PROMPT].reshape(1, SEQ, D_MODEL), x[T_PROMPT:].reshape(DEC_BATCH, DEC_SEQ, D_MODEL),
            jnp.stack(kp_out), jnp.stack(vp_out), jnp.stack(srp_out), jnp.stack(sip_out),
            jnp.stack(ks_out), jnp.stack(vs_out), jnp.stack(srs_out), jnp.stack(sis_out))
```
